```python
import jax, jax.numpy as jnp
from jax import lax
import numpy as np

D_MODEL = 1024
BATCH = 4
SEQ = 8192
DEPTH = 2
DEC_BATCH = 32
DEC_SEQ = 4
PAST_LEN = 16384
PAGE_SIZE = 128

D_MIX = D_MODEL
HEAD_DIM = 64
A_WIDTH = D_MIX // 4
A_HEADS = A_WIDTH // HEAD_DIM
CHUNK = 128
B_WIDTH = D_MIX // 4
POOL_WINDOWS = (2, 4, 8, 16)
N_POOL = len(POOL_WINDOWS)
POOL_GW = B_WIDTH // N_POOL
POOL_BUF = max(POOL_WINDOWS) - 1
C_WIDTH = D_MIX - A_WIDTH - B_WIDTH
C_HEADS = C_WIDTH // HEAD_DIM
Q_BLOCK = 128
PROJ_SPLITS = (A_WIDTH, A_WIDTH, B_WIDTH, C_WIDTH, C_WIDTH, C_WIDTH, C_HEADS)
D_PROJ = sum(PROJ_SPLITS)
D_FF = ((8 * D_MODEL // 3 + 127) // 128) * 128
FFN_RES = 0.5
RMS_EPS = 1e-6
FORGET_BIAS_INIT = 2.0

kernel_name = 'hymba_chunkmlp_pool_fox_decoder_step'


def rmsnorm(x, g):
    xf = x.astype(jnp.float32)
    y = xf * lax.rsqrt(jnp.mean(xf * xf, axis=-1, keepdims=True) + RMS_EPS)
    return (y * g.astype(jnp.float32)).astype(x.dtype)


def half_ffn(x, g_pre, g_post, w_gate, w_up, w_down):
    h = rmsnorm(x, g_pre)
    f = (jax.nn.silu(h @ w_gate) * (h @ w_up)) @ w_down
    return x + FFN_RES * rmsnorm(f, g_post)


def split_proj(p):
    offs = np.cumsum(PROJ_SPLITS)[:-1].tolist()
    return jnp.split(p, offs, axis=-1)


def to_heads(t):
    return t.reshape(t.shape[:-1] + (C_HEADS, HEAD_DIM))


def chunk_gate(u, v, g, w_s, b):
    bsz, n, _ = u.shape
    vn = rmsnorm(v.reshape(bsz, n, A_HEADS, HEAD_DIM), g.reshape(A_HEADS, HEAD_DIM))
    L = min(n, CHUNK)
    vc = vn.reshape(bsz, n // L, L, A_HEADS, HEAD_DIM)
    w = jnp.where(jnp.tril(jnp.ones((CHUNK, CHUNK), dtype=bool)), w_s, 0)[:, :L, :L]
    z = jnp.einsum('hts,bcshd->bcthd', w, vc) + b[:, :L].T[None, None, :, :, None]
    return u * z.reshape(bsz, n, A_WIDTH), vn.reshape(bsz, n, A_WIDTH)


def pool_mix(p_ext, n_prev, pos, w, scale):
    L = p_ext.shape[1]
    W = max(POOL_WINDOWS)
    pf = p_ext.astype(jnp.float32)
    cs = jnp.pad(jnp.cumsum(pf, axis=1), ((0, 0), (W, 0), (0, 0)))
    hi = cs[:, W + n_prev:W + L]
    outs = []
    for gi, win in enumerate(POOL_WINDOWS):
        sl = slice(gi * POOL_GW, (gi + 1) * POOL_GW)
        lo = cs[:, W + n_prev - win:W + L - win, sl]
        cnt = jnp.minimum(pos + 1, win).astype(jnp.float32)[None, :, None]
        d = (hi[..., sl] - lo) / cnt - pf[:, n_prev:, sl]
        outs.append(d @ w[gi].astype(jnp.float32))
    return (jnp.concatenate(outs, axis=-1) * scale.astype(jnp.float32)).astype(p_ext.dtype)


def forget_attend(q, c_q, qpos, k, v, c_k, kpos):
    s = jnp.einsum('bqhd,bkhd->bhqk', q, k, preferred_element_type=jnp.float32) * (HEAD_DIM ** -0.5)
    s = s + jnp.transpose(c_q, (0, 2, 1))[:, :, :, None] - jnp.transpose(c_k, (0, 2, 1))[:, :, None, :]
    s = jnp.where(kpos[None, :] <= qpos[:, None], s, -jnp.inf)
    p = jax.nn.softmax(s, axis=-1)
    return jnp.einsum('bhqk,bkhd->bqhd', p.astype(v.dtype), v)


def forget_attn_prompt(q, k, v, logf):
    bsz, T = q.shape[0], q.shape[1]
    c = jnp.cumsum(logf.astype(jnp.float32), axis=1)
    kpos = jnp.arange(T)

    def block(i):
        s0 = i * Q_BLOCK
        qb = lax.dynamic_slice_in_dim(q, s0, Q_BLOCK, axis=1)
        cb = lax.dynamic_slice_in_dim(c, s0, Q_BLOCK, axis=1)
        return forget_attend(qb, cb, s0 + jnp.arange(Q_BLOCK), k, v, c, kpos)

    o = lax.map(block, jnp.arange(T // Q_BLOCK))
    return jnp.transpose(o, (1, 0, 2, 3, 4)).reshape(bsz, T, C_WIDTH)


def forget_attn_sample(q, k_new, v_new, logf_new, k_pages, v_pages, lf_pages, page_table):
    db, n = q.shape[0], q.shape[1]
    kp = k_pages[page_table].reshape(db, -1, C_HEADS, HEAD_DIM)
    vp = v_pages[page_table].reshape(db, -1, C_HEADS, HEAD_DIM)
    lp = lf_pages[page_table].reshape(db, -1, C_HEADS)
    past = kp.shape[1]
    k = jnp.concatenate([kp, k_new.astype(kp.dtype)], axis=1)
    v = jnp.concatenate([vp, v_new.astype(vp.dtype)], axis=1)
    c = jnp.cumsum(jnp.concatenate([lp.astype(jnp.float32), logf_new.astype(jnp.float32)], axis=1), axis=1)
    o = forget_attend(q, c[:, past:], past + jnp.arange(n), k, v, c, jnp.arange(past + n))
    return o.reshape(db, n, C_WIDTH)


def mix_prompt(h, w_in_l, w_out_l, cvg, cws, cb, pw, ps, fb):
    T = h.shape[1]
    u, va, pin, q, k, v, fl = split_proj(h @ w_in_l)
    logf = jax.nn.log_sigmoid(fl.astype(jnp.float32) + fb.astype(jnp.float32))
    a, _ = chunk_gate(u, va, cvg, cws, cb)
    b = pool_mix(pin, 0, jnp.arange(T), pw, ps)
    kh, vh = to_heads(k), to_heads(v)
    c = forget_attn_prompt(to_heads(q), kh, vh, logf)
    y = jnp.concatenate([a, b, c.astype(a.dtype)], axis=-1) @ w_out_l
    return y, kh, vh, logf, pin[:, -POOL_BUF:]


def mix_sample(h, k_pages, v_pages, lf_pages, pool_buf, page_table, w_in_l, w_out_l, cvg, cws, cb, pw, ps, fb):
    n = h.shape[1]
    past = page_table.shape[1] * k_pages.shape[1]
    u, va, pin, q, k, v, fl = split_proj(h @ w_in_l)
    logf = jax.nn.log_sigmoid(fl.astype(jnp.float32) + fb.astype(jnp.float32))
    a, vn = chunk_gate(u, va, cvg, cws, cb)
    ext = jnp.concatenate([pool_buf.astype(pin.dtype), pin], axis=1)
    b = pool_mix(ext, pool_buf.shape[1], past + jnp.arange(n), pw, ps)
    kh, vh = to_heads(k), to_heads(v)
    c = forget_attn_sample(to_heads(q), kh, vh, logf, k_pages, v_pages, lf_pages, page_table)
    y = jnp.concatenate([a, b, c.astype(a.dtype)], axis=-1) @ w_out_l
    return y, kh, vh, logf, ext[:, -POOL_BUF:], vn


def setup_inputs(seed: int = 0) -> dict:
    key = jax.random.key(seed)
    keys = list(jax.random.split(key, 40))

    def nrm(shape, s):
        return s * jax.random.normal(keys.pop(), shape, jnp.float32)

    n_pages = PAST_LEN // PAGE_SIZE
    n_used = DEC_BATCH * n_pages
    n_phys = n_used + n_used // 4
    perm = jax.random.permutation(keys.pop(), n_phys)
    page_table = perm[:n_used].reshape(DEC_BATCH, n_pages).astype(jnp.int32)

    def gain(width):
        return 1.0 + nrm((DEPTH, width), 0.05)

    return {
        'x_prompt': nrm((BATCH, SEQ, D_MODEL), 1.0),
        'x_sample': nrm((DEC_BATCH, DEC_SEQ, D_MODEL), 1.0),
        'cache_k': nrm((DEPTH, n_phys, PAGE_SIZE, C_HEADS, HEAD_DIM), 1.0),
        'cache_v': nrm((DEPTH, n_phys, PAGE_SIZE, C_HEADS, HEAD_DIM), 1.0),
        'cache_logf': jax.nn.log_sigmoid(FORGET_BIAS_INIT + nrm((DEPTH, n_phys, PAGE_SIZE, C_HEADS), 1.0)),
        'state_pool': nrm((DEPTH, DEC_BATCH, POOL_BUF, B_WIDTH), 1.0),
        'page_table': page_table,
        'ffn1_g_pre': gain(D_MODEL),
        'ffn1_g_post': gain(D_MODEL),
        'ffn1_w_gate': nrm((DEPTH, D_MODEL, D_FF), D_MODEL ** -0.5),
        'ffn1_w_up': nrm((DEPTH, D_MODEL, D_FF), D_MODEL ** -0.5),
        'ffn1_w_down': nrm((DEPTH, D_FF, D_MODEL), D_FF ** -0.5),
        'mix_g_pre': gain(D_MODEL),
        'mix_g_post': gain(D_MODEL),
        'w_in': nrm((DEPTH, D_MODEL, D_PROJ), D_MODEL ** -0.5),
        'w_out': nrm((DEPTH, D_MIX, D_MODEL), D_MIX ** -0.5),
        'chunk_v_g': gain(A_WIDTH),
        'chunk_w_s': nrm((DEPTH, A_HEADS, CHUNK, CHUNK), CHUNK ** -0.5),
        'chunk_b': 1.0 + nrm((DEPTH, A_HEADS, CHUNK), 0.1),
        'pool_w': nrm((DEPTH, N_POOL, POOL_GW, POOL_GW), POOL_GW ** -0.5),
        'pool_scale': 1.0 + nrm((DEPTH, B_WIDTH), 0.1),
        'forget_b': FORGET_BIAS_INIT + nrm((DEPTH, C_HEADS), 0.1),
        'ffn2_g_pre': gain(D_MODEL),
        'ffn2_g_post': gain(D_MODEL),
        'ffn2_w_gate': nrm((DEPTH, D_MODEL, D_FF), D_MODEL ** -0.5),
        'ffn2_w_up': nrm((DEPTH, D_MODEL, D_FF), D_MODEL ** -0.5),
        'ffn2_w_down': nrm((DEPTH, D_FF, D_MODEL), D_FF ** -0.5),
    }


def reference(x_prompt, x_sample, cache_k, cache_v, cache_logf, state_pool, page_table,
              ffn1_g_pre, ffn1_g_post, ffn1_w_gate, ffn1_w_up, ffn1_w_down,
              mix_g_pre, mix_g_post, w_in, w_out,
              chunk_v_g, chunk_w_s, chunk_b, pool_w, pool_scale, forget_b,
              ffn2_g_pre, ffn2_g_post, ffn2_w_gate, ffn2_w_up, ffn2_w_down):
    xp, xs = x_prompt, x_sample
    kp_l, vp_l, lfp_l, poolp_l = [], [], [], []
    ks_l, vs_l, lfs_l, pools_l, cvs_l = [], [], [], [], []
    for l in range(DEPTH):
        f1 = (ffn1_g_pre[l], ffn1_g_post[l], ffn1_w_gate[l], ffn1_w_up[l], ffn1_w_down[l])
        f2 = (ffn2_g_pre[l], ffn2_g_post[l], ffn2_w_gate[l], ffn2_w_up[l], ffn2_w_down[l])
        mp = (w_in[l], w_out[l], chunk_v_g[l], chunk_w_s[l], chunk_b[l], pool_w[l], pool_scale[l], forget_b[l])
        xp = half_ffn(xp, *f1)
        xs = half_ffn(xs, *f1)
        yp, kp, vp, lfp, poolp = mix_prompt(rmsnorm(xp, mix_g_pre[l]), *mp)
        ys, ks, vs, lfs, pools, cvs = mix_sample(rmsnorm(xs, mix_g_pre[l]), cache_k[l], cache_v[l],
                                                 cache_logf[l], state_pool[l], page_table, *mp)
        xp = xp + rmsnorm(yp, mix_g_post[l])
        xs = xs + rmsnorm(ys, mix_g_post[l])
        xp = half_ffn(xp, *f2)
        xs = half_ffn(xs, *f2)
        kp_l.append(kp); vp_l.append(vp); lfp_l.append(lfp); poolp_l.append(poolp)
        ks_l.append(ks); vs_l.append(vs); lfs_l.append(lfs); pools_l.append(pools); cvs_l.append(cvs)
    y_prompt, y_sample = xp, xs
    k_prompt, v_prompt = jnp.stack(kp_l), jnp.stack(vp_l)
    logf_prompt, pool_prompt = jnp.stack(lfp_l), jnp.stack(poolp_l)
    k_sample, v_sample = jnp.stack(ks_l), jnp.stack(vs_l)
    logf_sample, pool_sample = jnp.stack(lfs_l), jnp.stack(pools_l)
    chunk_v_sample = jnp.stack(cvs_l)
    return (y_prompt, y_sample, k_prompt, v_prompt, logf_prompt, pool_prompt,
            k_sample, v_sample, logf_sample, pool_sample, chunk_v_sample)
```

```python
import functools

import jax
import jax.numpy as jnp
from jax import lax
from jax.experimental import pallas as pl
from jax.experimental.pallas import tpu as pltpu

F32 = jnp.float32
BF16 = jnp.bfloat16

RMS_EPS = 1e-6
FFN_RES = 0.5
HEAD_DIM = 64
CHUNK = 128
POOL_WINDOWS = (2, 4, 8, 16)
POOL_BUF = max(POOL_WINDOWS) - 1
POOL_CARRY = POOL_BUF + 1
A_WIDTH = 256
A_HEADS = A_WIDTH // HEAD_DIM
B_WIDTH = 256
C_HEADS = 8
C_WIDTH = C_HEADS * HEAD_DIM
LANES = 128
HEAD_PAIRS = C_WIDTH // LANES
NEW_PAD = 16
NEG_BIG = -1e30
VMEM_LIMIT = 56 * 1024 * 1024


def _resident(shape):
    nd = len(shape)
    return pl.BlockSpec(shape, lambda *_: (0,) * nd, pipeline_mode=pl.Buffered(1))


def _rms(x, g):
    return x * lax.rsqrt(jnp.mean(x * x, axis=-1, keepdims=True) + RMS_EPS) * g


def _split3(x):
    hi = x.astype(BF16)
    r = x - hi.astype(F32)
    mid = r.astype(BF16)
    lo = (r - mid.astype(F32)).astype(BF16)
    return hi, mid, lo


def _dot(a, b):
    return jnp.dot(a, b, preferred_element_type=F32)


def _dot_nt(a, b):
    return lax.dot_general(a, b, (((1,), (1,)), ((), ())), preferred_element_type=F32)


def _exact_rows_dot(x, ones_mat):
    rows = x.shape[0]
    hi, mid, lo = _split3(x)
    parts = jnp.concatenate([hi.astype(F32), mid.astype(F32), lo.astype(F32)], axis=0).astype(BF16)
    out = _dot(parts, ones_mat)
    return out[0:rows] + out[rows:2 * rows] + out[2 * rows:3 * rows]


def _ffn_kernel(x_ref, gpre_ref, gpost_ref, wg_ref, wu_ref, wd_ref, o_ref, *, f_chunks):
    x = x_ref[...]
    hb = _rms(x, gpre_ref[...]).astype(BF16)
    f = None
    for c0, c1 in f_chunks:
        g = _dot(hb, wg_ref[:, c0:c1])
        u = _dot(hb, wu_ref[:, c0:c1])
        act = (g / (1.0 + jnp.exp(-g)) * u).astype(BF16)
        part = _dot(act, wd_ref[c0:c1, :])
        f = part if f is None else f + part
    o_ref[...] = x + FFN_RES * _rms(f, gpost_ref[...])


def _ffn(x, gpre, gpost, wg, wu, wd, tm):
    n, d = x.shape
    f = wg.shape[1]
    half = (f // 2 + 255) // 256 * 256
    f_chunks = ((0, half), (half, f)) if half < f else ((0, f),)
    row = pl.BlockSpec((tm, d), lambda i: (i, 0))
    return pl.pallas_call(
        functools.partial(_ffn_kernel, f_chunks=f_chunks),
        grid=(n // tm,),
        in_specs=[row, _resident((1, d)), _resident((1, d)), _resident((d, f)), _resident((d, f)),
                  _resident((f, d))],
        out_specs=row,
        out_shape=jax.ShapeDtypeStruct((n, d), F32),
        compiler_params=pltpu.CompilerParams(dimension_semantics=("parallel",),
                                             vmem_limit_bytes=VMEM_LIMIT),
        name="half_ffn",
    )(x, gpre, gpost, wg, wu, wd)


def _log_sigmoid(z):
    return jnp.minimum(z, 0.0) - jnp.log1p(jnp.exp(-jnp.abs(z)))


def _head_rms(va, cvg):
    r = lax.broadcasted_iota(jnp.int32, (A_WIDTH, A_WIDTH), 0) // HEAD_DIM
    c = lax.broadcasted_iota(jnp.int32, (A_WIDTH, A_WIDTH), 1) // HEAD_DIM
    same_head = jnp.where(r == c, 1.0, 0.0).astype(BF16)
    x2 = va * va
    x2h = x2.astype(BF16)
    x2l = (x2 - x2h.astype(F32)).astype(BF16)
    ms = (_dot(x2h, same_head) + _dot(x2l, same_head)) * (1.0 / HEAD_DIM)
    return va * lax.rsqrt(ms + RMS_EPS) * cvg


def _gate_rows(w_heads, vnb, bias):
    rows = vnb.shape[0]
    lane_head = lax.broadcasted_iota(jnp.int32, (rows, A_WIDTH), 1) // HEAD_DIM
    z = bias
    for h in range(A_HEADS):
        z = z + jnp.where(lane_head == h, _dot(w_heads[h], vnb), 0.0)
    return z


def _pool_delta(delayed, count, half):
    w_small, w_big = POOL_WINDOWS[2 * half], POOL_WINDOWS[2 * half + 1]
    cur = delayed(0)
    acc = cur
    for j in range(1, w_small):
        acc = acc + delayed(j)
    acc_small = acc
    for j in range(w_small, w_big):
        acc = acc + delayed(j)
    is_small = lax.broadcasted_iota(jnp.int32, cur.shape, 1) < HEAD_DIM
    mean = jnp.where(is_small, acc_small, acc) / jnp.where(is_small, count(w_small), count(w_big))
    return mean - cur


def _proj_kernel(x_ref, gpre_ref, win_ref, wfl_ref, fb_ref, cvg_ref, ws_ref, bmat_ref, pbd_ref,
                 pscale_ref,
                 a_ref, b_ref, qt_ref, k2_ref, vt_ref, kout_ref, vout_ref, logf_ref, crow_ref,
                 tail_ref, ext_ref, carry_ref, *, tm):
    i = pl.program_id(1)
    x = x_ref[...]
    hb = _rms(x, gpre_ref[...]).astype(BF16)
    p = _dot(hb, win_ref[...])
    fl = _dot(hb, wfl_ref[...])
    u = p[:, 0:256]
    va = p[:, 256:512]
    pin = p[:, 512:768]
    q = p[:, 768:1280]
    k = p[:, 1280:1792]
    v = p[:, 1792:2304]

    kout_ref[...] = k
    vout_ref[...] = v
    k2_ref[...] = k.astype(BF16)
    for hp in range(HEAD_PAIRS):
        sl = slice(hp * LANES, (hp + 1) * LANES)
        qt_ref[0, hp] = (q[:, sl] * (HEAD_DIM ** -0.5)).T.astype(BF16)
        vt_ref[0, hp] = v[:, sl].T.astype(BF16)

    lf_t = _log_sigmoid(fl + fb_ref[...]).T[0:C_HEADS, :]
    logf_ref[0] = lf_t
    r = lax.broadcasted_iota(jnp.int32, (tm, tm), 0)
    c = lax.broadcasted_iota(jnp.int32, (tm, tm), 1)
    upto = jnp.where(r <= c, 1.0, 0.0).astype(BF16)

    @pl.when(i == 0)
    def _():
        carry_ref[...] = jnp.zeros_like(carry_ref)

    csum = _exact_rows_dot(lf_t, upto) + carry_ref[:, 0:1]
    crow_ref[0] = csum
    carry_ref[...] = jnp.broadcast_to(csum[:, tm - 1:tm], carry_ref.shape)

    vn = _head_rms(va, cvg_ref[...])
    vnb = vn.astype(BF16)
    rr = lax.broadcasted_iota(jnp.int32, (CHUNK, CHUNK), 0)
    cc = lax.broadcasted_iota(jnp.int32, (CHUNK, CHUNK), 1)
    w_heads = [jnp.where(cc <= rr, ws_ref[h], 0.0).astype(BF16) for h in range(A_HEADS)]
    for ci in range(tm // CHUNK):
        rs = slice(ci * CHUNK, (ci + 1) * CHUNK)
        z = _gate_rows(w_heads, vnb[rs], bmat_ref[...])
        a_ref[rs, :] = (u[rs] * z).astype(BF16)

    @pl.when(i == 0)
    def _():
        ext_ref[0:POOL_CARRY, :] = jnp.zeros((POOL_CARRY, B_WIDTH), F32)

    ext_ref[POOL_CARRY:POOL_CARRY + tm, :] = pin
    pos = i * tm + lax.broadcasted_iota(jnp.int32, (tm, LANES), 0)
    d_halves = []
    for half in range(2):
        ls = slice(half * LANES, (half + 1) * LANES)
        d_halves.append(_pool_delta(
            lambda j: ext_ref[pl.ds(POOL_CARRY - j, tm), ls],
            lambda w: jnp.minimum(pos + 1, w).astype(F32), half))
    d = jnp.concatenate(d_halves, axis=1).astype(BF16)
    b_ref[...] = (_dot(d, pbd_ref[...]) * pscale_ref[...]).astype(BF16)
    tail = ext_ref[tm:tm + POOL_CARRY, :]
    tail_ref[0] = tail
    ext_ref[0:POOL_CARRY, :] = tail


def _proj_prompt(x, bsz, seq, gpre, win, wfl, fb, cvg, ws, bmat, pbd, pscale, tm):
    n, d = x.shape
    nt = seq // tm
    dp = win.shape[1]
    row = lambda w: pl.BlockSpec((tm, w), lambda b, i: (b * nt + i, 0))
    pair_t = pl.BlockSpec((1, HEAD_PAIRS, LANES, tm), lambda b, i: (b, 0, 0, i))
    head_t = pl.BlockSpec((1, C_HEADS, tm), lambda b, i: (b, 0, i))
    out_shapes = (
        jax.ShapeDtypeStruct((n, A_WIDTH), BF16),
        jax.ShapeDtypeStruct((n, B_WIDTH), BF16),
        jax.ShapeDtypeStruct((bsz, HEAD_PAIRS, LANES, seq), BF16),
        jax.ShapeDtypeStruct((n, C_WIDTH), BF16),
        jax.ShapeDtypeStruct((bsz, HEAD_PAIRS, LANES, seq), BF16),
        jax.ShapeDtypeStruct((n, C_WIDTH), F32),
        jax.ShapeDtypeStruct((n, C_WIDTH), F32),
        jax.ShapeDtypeStruct((bsz, C_HEADS, seq), F32),
        jax.ShapeDtypeStruct((bsz, C_HEADS, seq), F32),
        jax.ShapeDtypeStruct((bsz, POOL_CARRY, B_WIDTH), F32),
    )
    out_specs = (
        row(A_WIDTH), row(B_WIDTH), pair_t, row(C_WIDTH), pair_t, row(C_WIDTH), row(C_WIDTH),
        head_t, head_t,
        pl.BlockSpec((1, POOL_CARRY, B_WIDTH), lambda b, i: (b, 0, 0)),
    )
    return pl.pallas_call(
        functools.partial(_proj_kernel, tm=tm),
        grid=(bsz, nt),
        in_specs=[row(d), _resident((1, d)), _resident((d, dp)), _resident((d, LANES)),
                  _resident((1, LANES)), _resident((1, A_WIDTH)), _resident((A_HEADS, CHUNK, CHUNK)),
                  _resident((CHUNK, A_WIDTH)), _resident((B_WIDTH, B_WIDTH)), _resident((1, B_WIDTH))],
        out_specs=out_specs,
        out_shape=out_shapes,
        scratch_shapes=[pltpu.VMEM((tm + POOL_CARRY, B_WIDTH), F32), pltpu.VMEM((C_HEADS, LANES), F32)],
        compiler_params=pltpu.CompilerParams(dimension_semantics=("arbitrary", "arbitrary"),
                                             vmem_limit_bytes=VMEM_LIMIT),
        name="proj_mix",
    )(x, gpre, win, wfl, fb, cvg, ws, bmat, pbd, pscale)


def _attn_kernel(qt_ref, k2_ref, vt_ref, cq_ref, call_ref, o_ref, ckrep_ref, *, blk, seq):
    qi = pl.program_id(2)

    @pl.when(qi == 0)
    def _():
        def fill(jb, carry):
            off = pl.multiple_of(jb * LANES, LANES)
            for hh in range(2):
                rowv = call_ref[0, 0, hh:hh + 1, pl.ds(off, LANES)]
                ckrep_ref[hh, pl.ds(off, LANES), :] = jnp.broadcast_to(rowv, (LANES, LANES)).T
            return carry
        lax.fori_loop(0, seq // LANES, fill, 0)

    q2 = qt_ref[0, 0]
    row_head = lax.broadcasted_iota(jnp.int32, (LANES, blk), 0) // HEAD_DIM
    qh = [jnp.where(row_head == hh, q2, jnp.zeros_like(q2)) for hh in range(2)]
    cq = [cq_ref[0, 0, hh:hh + 1, :] for hh in range(2)]
    reps = blk // LANES

    def step(j, carry, masked):
        off = pl.multiple_of(j * blk, blk)
        kb = k2_ref[pl.ds(off, blk), :]
        new = []
        for hh in range(2):
            m_run, l_run, acc = carry[hh]
            ck = ckrep_ref[hh, pl.ds(off, blk), :]
            t = _dot(kb, qh[hh]) - jnp.concatenate([ck] * reps, axis=1)
            if masked:
                kpos = lax.broadcasted_iota(jnp.int32, (blk, blk), 0)
                qpos = lax.broadcasted_iota(jnp.int32, (blk, blk), 1)
                t = jnp.where(kpos <= qpos, t, NEG_BIG)
            m_new = jnp.maximum(m_run, jnp.max(t, axis=0, keepdims=True) + cq[hh])
            alpha = jnp.exp(m_run - m_new)
            pr = jnp.exp(t - (m_new - cq[hh]))
            l_new = alpha * l_run + jnp.sum(pr, axis=0, keepdims=True)
            vb = vt_ref[0, 0, hh * HEAD_DIM:(hh + 1) * HEAD_DIM, pl.ds(off, blk)]
            acc_new = alpha * acc + _dot(vb, pr.astype(BF16))
            new.append((m_new, l_new, acc_new))
        return tuple(new)

    init = tuple((jnp.full((1, blk), NEG_BIG, F32), jnp.zeros((1, blk), F32),
                  jnp.zeros((HEAD_DIM, blk), F32)) for _ in range(2))
    carry = lax.fori_loop(0, qi, lambda j, cr: step(j, cr, False), init)
    carry = step(qi, carry, True)
    out_t = jnp.concatenate([carry[hh][2] / carry[hh][1] for hh in range(2)], axis=0)
    o_ref[...] = out_t.T.astype(BF16)


def _attn_prompt(qt, k2, vt, crow, bsz, seq, blk):
    n = bsz * seq
    nq = seq // blk
    crow4 = crow.reshape(bsz, HEAD_PAIRS, 2, seq)
    return pl.pallas_call(
        functools.partial(_attn_kernel, blk=blk, seq=seq),
        grid=(bsz, HEAD_PAIRS, nq),
        in_specs=[
            pl.BlockSpec((1, 1, LANES, blk), lambda b, hp, qi: (b, hp, 0, qi)),
            pl.BlockSpec((seq, LANES), lambda b, hp, qi: (b, hp)),
            pl.BlockSpec((1, 1, LANES, seq), lambda b, hp, qi: (b, hp, 0, 0)),
            pl.BlockSpec((1, 1, 2, blk), lambda b, hp, qi: (b, hp, 0, qi)),
            pl.BlockSpec((1, 1, 2, seq), lambda b, hp, qi: (b, hp, 0, 0)),
        ],
        out_specs=pl.BlockSpec((blk, LANES), lambda b, hp, qi: (b * nq + qi, hp)),
        out_shape=jax.ShapeDtypeStruct((n, C_WIDTH), BF16),
        scratch_shapes=[pltpu.VMEM((2, seq, LANES), F32)],
        compiler_params=pltpu.CompilerParams(
            dimension_semantics=("arbitrary", "arbitrary", "arbitrary"), vmem_limit_bytes=VMEM_LIMIT),
        name="forget_attn",
    )(qt, k2, vt, crow4, crow4)


def _out_kernel(x_ref, a_ref, b_ref, c_ref, wout_ref, gpost_ref, o_ref):
    y = (_dot(a_ref[...], wout_ref[0:A_WIDTH, :])
         + _dot(b_ref[...], wout_ref[A_WIDTH:A_WIDTH + B_WIDTH, :])
         + _dot(c_ref[...], wout_ref[A_WIDTH + B_WIDTH:, :]))
    o_ref[...] = x_ref[...] + _rms(y, gpost_ref[...])


def _mix_out(x, a, b, c, wout, gpost, tm):
    n, d = x.shape
    row = lambda w: pl.BlockSpec((tm, w), lambda i: (i, 0))
    return pl.pallas_call(
        _out_kernel,
        grid=(n // tm,),
        in_specs=[row(d), row(A_WIDTH), row(B_WIDTH), row(C_WIDTH), _resident(wout.shape),
                  _resident((1, d))],
        out_specs=row(d),
        out_shape=jax.ShapeDtypeStruct((n, d), F32),
        compiler_params=pltpu.CompilerParams(dimension_semantics=("parallel",),
                                             vmem_limit_bytes=VMEM_LIMIT),
        name="mix_out",
    )(x, a, b, c, wout, gpost)


def _sproj_kernel(x_ref, gpre_ref, win_ref, wfl_ref, fb_ref, cvg_ref, wexp_ref, bsamp_ref, pbd_ref,
                  pscale_ref, state_ref,
                  a_ref, b_ref, q_ref, k_ref, v_ref, logf_ref, cnew_ref, vn_ref, pin_ref,
                  *, db, n_new, past):
    rows = db * n_new
    hb = _rms(x_ref[...], gpre_ref[...]).astype(BF16)
    p = _dot(hb, win_ref[...])
    fl = _dot(hb, wfl_ref[...])
    u = p[:, 0:256]
    va = p[:, 256:512]
    pin = p[:, 512:768]
    q_ref[...] = p[:, 768:1280]
    k_ref[...] = p[:, 1280:1792]
    v_ref[...] = p[:, 1792:2304]
    pin_ref[...] = pin

    lf = _log_sigmoid(fl + fb_ref[...])
    logf_ref[...] = lf[:, 0:C_HEADS]
    run = lf[0:db]
    cnew_ref[0:db, :] = run[:, 0:C_HEADS]
    for t in range(1, n_new):
        run = run + lf[t * db:(t + 1) * db]
        cnew_ref[t * db:(t + 1) * db, :] = run[:, 0:C_HEADS]

    vn = _head_rms(va, cvg_ref[...])
    vn_ref[...] = vn
    rr = lax.broadcasted_iota(jnp.int32, (rows, rows), 0)
    cc = lax.broadcasted_iota(jnp.int32, (rows, rows), 1)
    keep = ((rr % db) == (cc % db)) & ((cc // db) <= (rr // db))
    w_heads = [jnp.where(keep, wexp_ref[h], 0.0).astype(BF16) for h in range(A_HEADS)]
    z = _gate_rows(w_heads, vn.astype(BF16), bsamp_ref[...])
    a_ref[...] = (u * z).astype(BF16)

    def ext(idx, ls):
        if idx < POOL_BUF:
            return state_ref[idx][:, ls]
        t = idx - POOL_BUF
        return pin[t * db:(t + 1) * db, ls]

    d_rows = []
    for t in range(n_new):
        d_halves = []
        for half in range(2):
            ls = slice(half * LANES, (half + 1) * LANES)
            d_halves.append(_pool_delta(lambda j: ext(POOL_BUF + t - j, ls),
                                        lambda w: float(min(past + t + 1, w)), half))
        d_rows.append(jnp.concatenate(d_halves, axis=1))
    d = jnp.concatenate(d_rows, axis=0).astype(BF16)
    b_ref[...] = (_dot(d, pbd_ref[...]) * pscale_ref[...]).astype(BF16)


def _proj_sample(x, gpre, win, wfl, fb, cvg, wexp, bsamp, pbd, pscale, state_t, db, n_new, past):
    rows, d = x.shape
    out_shapes = (
        jax.ShapeDtypeStruct((rows, A_WIDTH), BF16),
        jax.ShapeDtypeStruct((rows, B_WIDTH), BF16),
        jax.ShapeDtypeStruct((rows, C_WIDTH), F32),
        jax.ShapeDtypeStruct((rows, C_WIDTH), F32),
        jax.ShapeDtypeStruct((rows, C_WIDTH), F32),
        jax.ShapeDtypeStruct((rows, C_HEADS), F32),
        jax.ShapeDtypeStruct((rows, C_HEADS), F32),
        jax.ShapeDtypeStruct((rows, A_WIDTH), F32),
        jax.ShapeDtypeStruct((rows, B_WIDTH), F32),
    )
    args = (x, gpre, win, wfl, fb, cvg, wexp, bsamp, pbd, pscale, state_t)
    return pl.pallas_call(
        functools.partial(_sproj_kernel, db=db, n_new=n_new, past=past),
        grid=(1,),
        in_specs=[pl.BlockSpec(a.shape, lambda i, nd=a.ndim: (0,) * nd) for a in args],
        out_specs=tuple(pl.BlockSpec(s.shape, lambda i: (0, 0)) for s in out_shapes),
        out_shape=out_shapes,
        compiler_params=pltpu.CompilerParams(dimension_semantics=("arbitrary",),
                                             vmem_limit_bytes=VMEM_LIMIT),
        name="proj_mix_sample",
    )(*args)


def _pattn_kernel(pt_ref, q_ref, knew_ref, vnew_ref, cq_ref, cknew_ref, *rest, group, n_new):
    k_refs = rest[:group]
    v_refs = rest[group:2 * group]
    lf_refs = rest[2 * group:3 * group]
    o_ref, qbd_ref, m_ref, l_ref, acc_ref, carry_ref = rest[3 * group:]
    g = pl.program_id(1)
    rows = n_new * C_HEADS
    row_head = lax.broadcasted_iota(jnp.int32, (rows, C_WIDTH), 0) % C_HEADS
    lane_head = lax.broadcasted_iota(jnp.int32, (rows, C_WIDTH), 1) // HEAD_DIM
    own = row_head == lane_head

    @pl.when(g == 0)
    def _():
        qs = q_ref[0] * (HEAD_DIM ** -0.5)
        qrep = jnp.concatenate([jnp.broadcast_to(qs[t:t + 1], (C_HEADS, C_WIDTH)) for t in range(n_new)],
                               axis=0)
        qbd_ref[...] = jnp.where(own, qrep, 0.0).astype(BF16)
        m_ref[...] = jnp.full_like(m_ref, NEG_BIG)
        l_ref[...] = jnp.zeros_like(l_ref)
        acc_ref[...] = jnp.zeros_like(acc_ref)
        carry_ref[...] = jnp.zeros_like(carry_ref)

    qbd = qbd_ref[...]
    cq = cq_ref[0]

    def update(scores, pv):
        m_run = m_ref[...]
        m_cur = m_run
        for s in scores:
            m_cur = jnp.maximum(m_cur, jnp.max(s, axis=1, keepdims=True))
        alpha = jnp.exp(m_run - m_cur)
        l_new = alpha * l_ref[...]
        acc = alpha[:, 0:1] * acc_ref[...]
        for i, s in enumerate(scores):
            pr = jnp.exp(s - m_cur[:, 0:1])
            l_new = l_new + jnp.sum(pr, axis=1, keepdims=True)
            acc = acc + pv(i, pr.astype(BF16))
        m_ref[...] = m_cur
        l_ref[...] = l_new
        acc_ref[...] = acc

    page = lf_refs[0].shape[-1]
    sr = lax.broadcasted_iota(jnp.int32, (page, 2 * page), 0)
    sc = lax.broadcasted_iota(jnp.int32, (page, 2 * page), 1)
    later_or_all = jnp.where((sr > sc) | (sc >= page), 1.0, 0.0).astype(BF16)
    sums = _exact_rows_dot(jnp.concatenate([lf_refs[i][0, 0] for i in range(group)], axis=0), later_or_all)
    carry = carry_ref[...]
    scores = []
    for i in range(group):
        hs = slice(i * C_HEADS, (i + 1) * C_HEADS)
        bias = sums[hs, 0:page] + carry
        carry = carry + sums[hs, page:2 * page]
        s = _dot(qbd, k_refs[i][0, 0].astype(BF16))
        scores.append(s + jnp.concatenate([bias] * n_new, axis=0) + cq)
    carry_ref[...] = carry
    update(scores, lambda i, pr: _dot_nt(pr, v_refs[i][0, 0].astype(BF16)))

    @pl.when(g == pl.num_programs(1) - 1)
    def _():
        s = _dot_nt(qbd, knew_ref[0].astype(BF16))
        s = s + cq[:, 0:NEW_PAD] - cknew_ref[0]
        tq = lax.broadcasted_iota(jnp.int32, (rows, NEW_PAD), 0) // C_HEADS
        sk = lax.broadcasted_iota(jnp.int32, (rows, NEW_PAD), 1)
        s = jnp.where(sk <= tq, s, NEG_BIG)
        update([s], lambda i, pr: _dot(pr, vnew_ref[0].astype(BF16)))
        full = jnp.where(own, acc_ref[...] / l_ref[...][:, 0:1], 0.0)
        o_ref[0] = jnp.sum(full.reshape(n_new, C_HEADS, C_WIDTH), axis=1).astype(BF16)


def _paged_attn(page_table, q_b, knew_b, vnew_b, cq_rep, cknew, cache_kt, cache_vt, cache_lft, layer, group):
    db, n_pages = page_table.shape
    n_new = q_b.shape[1]
    page = cache_kt.shape[-1]
    rows = n_new * C_HEADS
    per_seq = lambda shape: pl.BlockSpec((1,) + shape, lambda b, g, pt: (b,) + (0,) * len(shape))

    def page_spec(i, height):
        def index(b, g, pt):
            return (layer, pt[b, n_pages - 1 - (g * group + i)], 0, 0)
        return pl.BlockSpec((1, 1, height, page), index)

    in_specs = [per_seq((n_new, C_WIDTH)), per_seq(knew_b.shape[1:]), per_seq(vnew_b.shape[1:]),
                per_seq((rows, LANES)), per_seq(cknew.shape[1:])]
    in_specs += [page_spec(i, C_WIDTH) for i in range(group)] * 2
    in_specs += [page_spec(i, C_HEADS) for i in range(group)]
    grid_spec = pltpu.PrefetchScalarGridSpec(
        num_scalar_prefetch=1,
        grid=(db, n_pages // group),
        in_specs=in_specs,
        out_specs=pl.BlockSpec((1, n_new, C_WIDTH), lambda b, g, pt: (b, 0, 0)),
        scratch_shapes=[pltpu.VMEM((rows, C_WIDTH), BF16), pltpu.VMEM((rows, LANES), F32),
                        pltpu.VMEM((rows, LANES), F32), pltpu.VMEM((rows, C_WIDTH), F32),
                        pltpu.VMEM((C_HEADS, page), F32)],
    )
    return pl.pallas_call(
        functools.partial(_pattn_kernel, group=group, n_new=n_new),
        grid_spec=grid_spec,
        out_shape=jax.ShapeDtypeStruct((db, n_new, C_WIDTH), BF16),
        compiler_params=pltpu.CompilerParams(dimension_semantics=("arbitrary", "arbitrary"),
                                             vmem_limit_bytes=VMEM_LIMIT),
        name="paged_forget_attn",
    )(page_table, q_b, knew_b, vnew_b, cq_rep, cknew,
      *([cache_kt] * group), *([cache_vt] * group), *([cache_lft] * group))


def _row_tile(n, target):
    t = min(n, target)
    while n % t:
        t //= 2
    return t


def kernel(x_prompt, x_sample, cache_k, cache_v, cache_logf, state_pool, page_table,
           ffn1_g_pre, ffn1_g_post, ffn1_w_gate, ffn1_w_up, ffn1_w_down,
           mix_g_pre, mix_g_post, w_in, w_out,
           chunk_v_g, chunk_w_s, chunk_b, pool_w, pool_scale, forget_b,
           ffn2_g_pre, ffn2_g_post, ffn2_w_gate, ffn2_w_up, ffn2_w_down):
    bsz, seq, d = x_prompt.shape
    db, n_new, _ = x_sample.shape
    depth, n_phys, page = cache_k.shape[:3]
    n_pages = page_table.shape[1]
    past = n_pages * page
    n = bsz * seq
    rows_s = db * n_new
    d_main = w_in.shape[-1] - C_HEADS

    tm_ffn = _row_tile(n, 512)
    tm_proj = _row_tile(seq, 512)
    blk = _row_tile(seq, 256)
    group = _row_tile(n_pages, 8)

    xp = x_prompt.reshape(n, d)
    xs = jnp.transpose(x_sample, (1, 0, 2)).reshape(rows_s, d)
    cache_kt = jnp.transpose(cache_k, (0, 1, 3, 4, 2)).reshape(depth, n_phys, C_WIDTH, page)
    cache_vt = jnp.transpose(cache_v, (0, 1, 3, 4, 2)).reshape(depth, n_phys, C_WIDTH, page)
    cache_lft = jnp.transpose(cache_logf, (0, 1, 3, 2))

    outs = {k: [] for k in ("kp", "vp", "lfp", "poolp", "ks", "vs", "lfs", "pools", "cvs")}
    for l in range(depth):
        row2 = lambda v: v[l].reshape(1, -1)
        f1 = (row2(ffn1_g_pre), row2(ffn1_g_post), ffn1_w_gate[l].astype(BF16),
              ffn1_w_up[l].astype(BF16), ffn1_w_down[l].astype(BF16))
        f2 = (row2(ffn2_g_pre), row2(ffn2_g_post), ffn2_w_gate[l].astype(BF16),
              ffn2_w_up[l].astype(BF16), ffn2_w_down[l].astype(BF16))
        win = w_in[l, :, :d_main].astype(BF16)
        wfl = jnp.pad(w_in[l, :, d_main:], ((0, 0), (0, LANES - C_HEADS))).astype(BF16)
        fb = jnp.pad(forget_b[l], (0, LANES - C_HEADS)).reshape(1, LANES)
        wout = w_out[l].astype(BF16)
        bmat = jnp.repeat(chunk_b[l].T, HEAD_DIM, axis=1)
        pbd = jax.scipy.linalg.block_diag(*[pool_w[l, g] for g in range(len(POOL_WINDOWS))]).astype(BF16)
        mixp = (row2(mix_g_pre), win, wfl, fb, row2(chunk_v_g))
        pool_p = (pbd, row2(pool_scale))

        xp = _ffn(xp, *f1, tm_ffn)
        (a, b, qt, k2, vt, kf, vf, lf_t, crow, tail) = _proj_prompt(
            xp, bsz, seq, *mixp, chunk_w_s[l], bmat, *pool_p, tm_proj)
        c = _attn_prompt(qt, k2, vt, crow, bsz, seq, blk)
        xp = _mix_out(xp, a, b, c, wout, row2(mix_g_post), tm_ffn)
        xp = _ffn(xp, *f2, tm_ffn)
        outs["kp"].append(kf.reshape(bsz, seq, C_HEADS, HEAD_DIM))
        outs["vp"].append(vf.reshape(bsz, seq, C_HEADS, HEAD_DIM))
        outs["lfp"].append(jnp.transpose(lf_t, (0, 2, 1)))
        outs["poolp"].append(tail[:, POOL_CARRY - POOL_BUF:])

        xs = _ffn(xs, *f1, rows_s)
        wexp = jnp.repeat(jnp.repeat(chunk_w_s[l, :, :n_new, :n_new], db, axis=1), db, axis=2)
        bsamp = jnp.repeat(bmat[:n_new], db, axis=0)
        state_t = jnp.transpose(state_pool[l], (1, 0, 2))
        (a_s, b_s, q_s, k_s, v_s, lf_s, cnew, vn_s, pin_s) = _proj_sample(
            xs, *mixp, wexp, bsamp, *pool_p, state_t, db, n_new, past)
        to_b = lambda v: jnp.transpose(v.reshape(n_new, db, -1), (1, 0, 2))
        q_b, k_b, v_b, cnew_b = to_b(q_s), to_b(k_s), to_b(v_s), to_b(cnew)
        pad_new = ((0, 0), (0, NEW_PAD - n_new), (0, 0))
        cq_rep = jnp.broadcast_to(cnew_b.reshape(db, n_new * C_HEADS, 1), (db, n_new * C_HEADS, LANES))
        cknew = jnp.pad(jnp.tile(jnp.transpose(cnew_b, (0, 2, 1)), (1, n_new, 1)),
                        ((0, 0), (0, 0), (0, NEW_PAD - n_new)))
        c_s = _paged_attn(page_table, q_b, jnp.pad(k_b, pad_new), jnp.pad(v_b, pad_new), cq_rep, cknew,
                          cache_kt, cache_vt, cache_lft, l, group)
        c_s = jnp.transpose(c_s, (1, 0, 2)).reshape(rows_s, C_WIDTH)
        xs = _mix_out(xs, a_s, b_s, c_s, wout, row2(mix_g_post), rows_s)
        xs = _ffn(xs, *f2, rows_s)
        outs["ks"].append(k_b.reshape(db, n_new, C_HEADS, HEAD_DIM))
        outs["vs"].append(v_b.reshape(db, n_new, C_HEADS, HEAD_DIM))
        outs["lfs"].append(to_b(lf_s))
        outs["pools"].append(jnp.concatenate([state_pool[l], to_b(pin_s)], axis=1)[:, -POOL_BUF:])
        outs["cvs"].append(to_b(vn_s))

    y_prompt = xp.reshape(bsz, seq, d)
    y_sample = jnp.transpose(xs.reshape(n_new, db, d), (1, 0, 2))
    st = lambda key: jnp.stack(outs[key])
    return (y_prompt, y_sample, st("kp"), st("vp"), st("lfp"), st("poolp"),
            st("ks"), st("vs"), st("lfs"), st("pools"), st("cvs"))
```

```python
import functools

import jax
import jax.numpy as jnp
from jax import lax
from jax.experimental import pallas as pl
from jax.experimental.pallas import tpu as pltpu

F32 = jnp.float32
BF16 = jnp.bfloat16

RMS_EPS = 1e-6
FFN_RES = 0.5
HEAD_DIM = 64
CHUNK = 128
POOL_WINDOWS = (2, 4, 8, 16)
POOL_BUF = max(POOL_WINDOWS) - 1
POOL_CARRY = POOL_BUF + 1
A_WIDTH = 256
A_HEADS = A_WIDTH // HEAD_DIM
B_WIDTH = 256
C_HEADS = 8
C_WIDTH = C_HEADS * HEAD_DIM
LANES = 128
HEAD_PAIRS = C_WIDTH // LANES
NEW_PAD = 16
DENOM_ROWS = 16
LOG2E = 1.4426950408889634
NEG_BIG = -1e30
VMEM_LIMIT = 56 * 1024 * 1024


def _resident(shape):
    nd = len(shape)
    return pl.BlockSpec(shape, lambda *_: (0,) * nd, pipeline_mode=pl.Buffered(1))


def _rms(x, g):
    return x * lax.rsqrt(jnp.mean(x * x, axis=-1, keepdims=True) + RMS_EPS) * g


def _split3(x):
    hi = x.astype(BF16)
    r = x - hi.astype(F32)
    mid = r.astype(BF16)
    lo = (r - mid.astype(F32)).astype(BF16)
    return hi, mid, lo


def _dot(a, b):
    return jnp.dot(a, b, preferred_element_type=F32)


def _dot_nt(a, b):
    return lax.dot_general(a, b, (((1,), (1,)), ((), ())), preferred_element_type=F32)


def _exact_rows_dot(x, ones_mat):
    rows = x.shape[0]
    hi, mid, lo = _split3(x)
    parts = jnp.concatenate([hi.astype(F32), mid.astype(F32), lo.astype(F32)], axis=0).astype(BF16)
    out = _dot(parts, ones_mat)
    return out[0:rows] + out[rows:2 * rows] + out[2 * rows:3 * rows]


def _ffn_kernel(x_ref, gpre_ref, gpost_ref, wg_ref, wu_ref, wd_ref, o_ref, *, f_chunks):
    x = x_ref[...]
    hb = _rms(x, gpre_ref[...]).astype(BF16)
    f = None
    for c0, c1 in f_chunks:
        g = _dot(hb, wg_ref[:, c0:c1])
        u = _dot(hb, wu_ref[:, c0:c1])
        act = (g / (1.0 + jnp.exp(-g)) * u).astype(BF16)
        part = _dot(act, wd_ref[c0:c1, :])
        f = part if f is None else f + part
    o_ref[...] = x + FFN_RES * _rms(f, gpost_ref[...])


def _ffn(x, gpre, gpost, wg, wu, wd, tm):
    n, d = x.shape
    f = wg.shape[1]
    half = (f // 2 + 255) // 256 * 256
    f_chunks = ((0, half), (half, f)) if half < f else ((0, f),)
    row = pl.BlockSpec((tm, d), lambda i: (i, 0))
    return pl.pallas_call(
        functools.partial(_ffn_kernel, f_chunks=f_chunks),
        grid=(n // tm,),
        in_specs=[row, _resident((1, d)), _resident((1, d)), _resident((d, f)), _resident((d, f)),
                  _resident((f, d))],
        out_specs=row,
        out_shape=jax.ShapeDtypeStruct((n, d), F32),
        compiler_params=pltpu.CompilerParams(dimension_semantics=("parallel",),
                                             vmem_limit_bytes=VMEM_LIMIT),
        name="half_ffn",
    )(x, gpre, gpost, wg, wu, wd)


def _log_sigmoid(z):
    return jnp.minimum(z, 0.0) - jnp.log1p(jnp.exp(-jnp.abs(z)))


def _head_rms(va, cvg):
    r = lax.broadcasted_iota(jnp.int32, (A_WIDTH, A_WIDTH), 0) // HEAD_DIM
    c = lax.broadcasted_iota(jnp.int32, (A_WIDTH, A_WIDTH), 1) // HEAD_DIM
    same_head = jnp.where(r == c, 1.0, 0.0).astype(BF16)
    x2 = va * va
    x2h = x2.astype(BF16)
    x2l = (x2 - x2h.astype(F32)).astype(BF16)
    ms = (_dot(x2h, same_head) + _dot(x2l, same_head)) * (1.0 / HEAD_DIM)
    return va * lax.rsqrt(ms + RMS_EPS) * cvg


def _gate_rows(w_heads, vnb, bias):
    rows = vnb.shape[0]
    lane_head = lax.broadcasted_iota(jnp.int32, (rows, A_WIDTH), 1) // HEAD_DIM
    z = bias
    for h in range(A_HEADS):
        z = z + jnp.where(lane_head == h, _dot(w_heads[h], vnb), 0.0)
    return z


def _pool_delta(delayed, count, half):
    w_small, w_big = POOL_WINDOWS[2 * half], POOL_WINDOWS[2 * half + 1]
    cur = delayed(0)
    acc = cur
    for j in range(1, w_small):
        acc = acc + delayed(j)
    acc_small = acc
    for j in range(w_small, w_big):
        acc = acc + delayed(j)
    is_small = lax.broadcasted_iota(jnp.int32, cur.shape, 1) < HEAD_DIM
    mean = jnp.where(is_small, acc_small, acc) / jnp.where(is_small, count(w_small), count(w_big))
    return mean - cur


def _proj_kernel(x_ref, gpre_ref, win_ref, wfl_ref, fb_ref, cvg_ref, ws_ref, bmat_ref, pbd_ref,
                 pscale_ref,
                 a_ref, b_ref, qt_ref, k2_ref, kc_ref, vt_ref, kout_ref, vout_ref, logf_ref, crow_ref,
                 tail_ref, ext_ref, carry_ref, *, tm):
    i = pl.program_id(1)
    x = x_ref[...]
    hb = _rms(x, gpre_ref[...]).astype(BF16)
    p = _dot(hb, win_ref[...])
    fl = _dot(hb, wfl_ref[...])
    u = p[:, 0:256]
    va = p[:, 256:512]
    pin = p[:, 512:768]
    q = p[:, 768:1280]
    k = p[:, 1280:1792]
    v = p[:, 1792:2304]

    k2_ref[...] = k.astype(BF16)
    for hp in range(HEAD_PAIRS):
        sl = slice(hp * LANES, (hp + 1) * LANES)
        qt_ref[0, hp] = (q[:, sl] * (HEAD_DIM ** -0.5 * LOG2E)).T.astype(BF16)
        kout_ref[0, hp] = k[:, sl].T
        v_t = v[:, sl].T
        vout_ref[0, hp] = v_t
        vt_ref[0, hp] = v_t.astype(BF16)

    lf_t = _log_sigmoid(fl + fb_ref[...]).T[0:C_HEADS, :]
    logf_ref[0] = lf_t
    r = lax.broadcasted_iota(jnp.int32, (tm, tm), 0)
    c = lax.broadcasted_iota(jnp.int32, (tm, tm), 1)
    upto = jnp.where(r <= c, 1.0, 0.0).astype(BF16)

    @pl.when(i == 0)
    def _():
        carry_ref[...] = jnp.zeros_like(carry_ref)

    csum = _exact_rows_dot(lf_t, upto) + carry_ref[:, 0:1]
    carry_ref[...] = jnp.broadcast_to(csum[:, tm - 1:tm], carry_ref.shape)
    csum2 = csum * LOG2E
    crow_ref[0] = csum2
    hi, mid, lo = _split3(csum2)
    cols = jnp.concatenate([hi.astype(F32), mid.astype(F32), lo.astype(F32),
                            jnp.zeros((LANES - 3 * C_HEADS, tm), F32)], axis=0)
    kc_ref[...] = cols.T.astype(BF16)

    vn = _head_rms(va, cvg_ref[...])
    vnb = vn.astype(BF16)
    rr = lax.broadcasted_iota(jnp.int32, (CHUNK, CHUNK), 0)
    cc = lax.broadcasted_iota(jnp.int32, (CHUNK, CHUNK), 1)
    w_heads = [jnp.where(cc <= rr, ws_ref[h], 0.0).astype(BF16) for h in range(A_HEADS)]
    for ci in range(tm // CHUNK):
        rs = slice(ci * CHUNK, (ci + 1) * CHUNK)
        z = _gate_rows(w_heads, vnb[rs], bmat_ref[...])
        a_ref[rs, :] = (u[rs] * z).astype(BF16)

    @pl.when(i == 0)
    def _():
        ext_ref[0:POOL_CARRY, :] = jnp.zeros((POOL_CARRY, B_WIDTH), F32)

    ext_ref[POOL_CARRY:POOL_CARRY + tm, :] = pin
    pos = i * tm + lax.broadcasted_iota(jnp.int32, (tm, LANES), 0)
    d_halves = []
    for half in range(2):
        ls = slice(half * LANES, (half + 1) * LANES)
        d_halves.append(_pool_delta(
            lambda j: ext_ref[pl.ds(POOL_CARRY - j, tm), ls],
            lambda w: jnp.minimum(pos + 1, w).astype(F32), half))
    d = jnp.concatenate(d_halves, axis=1).astype(BF16)
    b_ref[...] = (_dot(d, pbd_ref[...]) * pscale_ref[...]).astype(BF16)
    tail = ext_ref[tm:tm + POOL_CARRY, :]
    tail_ref[0] = tail
    ext_ref[0:POOL_CARRY, :] = tail


def _proj_prompt(x, bsz, seq, gpre, win, wfl, fb, cvg, ws, bmat, pbd, pscale, tm):
    n, d = x.shape
    nt = seq // tm
    dp = win.shape[1]
    row = lambda w: pl.BlockSpec((tm, w), lambda b, i: (b * nt + i, 0))
    pair_t = pl.BlockSpec((1, HEAD_PAIRS, LANES, tm), lambda b, i: (b, 0, 0, i))
    head_t = pl.BlockSpec((1, C_HEADS, tm), lambda b, i: (b, 0, i))
    pair_shape = (bsz, HEAD_PAIRS, LANES, seq)
    out_shapes = (
        jax.ShapeDtypeStruct((n, A_WIDTH), BF16),
        jax.ShapeDtypeStruct((n, B_WIDTH), BF16),
        jax.ShapeDtypeStruct(pair_shape, BF16),
        jax.ShapeDtypeStruct((n, C_WIDTH), BF16),
        jax.ShapeDtypeStruct((n, LANES), BF16),
        jax.ShapeDtypeStruct(pair_shape, BF16),
        jax.ShapeDtypeStruct(pair_shape, F32),
        jax.ShapeDtypeStruct(pair_shape, F32),
        jax.ShapeDtypeStruct((bsz, C_HEADS, seq), F32),
        jax.ShapeDtypeStruct((bsz, C_HEADS, seq), F32),
        jax.ShapeDtypeStruct((bsz, POOL_CARRY, B_WIDTH), F32),
    )
    out_specs = (
        row(A_WIDTH), row(B_WIDTH), pair_t, row(C_WIDTH), row(LANES), pair_t, pair_t, pair_t,
        head_t, head_t,
        pl.BlockSpec((1, POOL_CARRY, B_WIDTH), lambda b, i: (b, 0, 0)),
    )
    return pl.pallas_call(
        functools.partial(_proj_kernel, tm=tm),
        grid=(bsz, nt),
        in_specs=[row(d), _resident((1, d)), _resident((d, dp)), _resident((d, LANES)),
                  _resident((1, LANES)), _resident((1, A_WIDTH)), _resident((A_HEADS, CHUNK, CHUNK)),
                  _resident((CHUNK, A_WIDTH)), _resident((B_WIDTH, B_WIDTH)), _resident((1, B_WIDTH))],
        out_specs=out_specs,
        out_shape=out_shapes,
        scratch_shapes=[pltpu.VMEM((tm + POOL_CARRY, B_WIDTH), F32), pltpu.VMEM((C_HEADS, LANES), F32)],
        compiler_params=pltpu.CompilerParams(dimension_semantics=("arbitrary", "arbitrary"),
                                             vmem_limit_bytes=VMEM_LIMIT),
        name="proj_mix",
    )(x, gpre, win, wfl, fb, cvg, ws, bmat, pbd, pscale)


def _attn_kernel(qt_ref, k2_ref, kc_ref, vt_ref, cq_ref, o_ref, *, blk):
    hp = pl.program_id(1)
    qi = pl.program_id(2)
    q2 = qt_ref[0, 0]
    row = lax.broadcasted_iota(jnp.int32, (LANES, blk), 0)
    qa, cq = [], []
    for hh in range(2):
        own_q = jnp.where((row // HEAD_DIM) == hh, q2, jnp.zeros_like(q2))
        minus_ck = jnp.where((row < 3 * C_HEADS) & ((row % C_HEADS) == 2 * hp + hh), -1.0, 0.0)
        qa.append(jnp.concatenate([own_q, minus_ck.astype(BF16)], axis=0))
        cq.append(cq_ref[0, 0, hh:hh + 1, :])
    ones = jnp.ones((DENOM_ROWS, blk), BF16)

    def step(j, carry, masked):
        off = pl.multiple_of(j * blk, blk)
        kk = jnp.concatenate([k2_ref[pl.ds(off, blk), :], kc_ref[pl.ds(off, blk), :]], axis=1)
        new = []
        for hh in range(2):
            m_run, acc = carry[hh]
            t = _dot(kk, qa[hh])
            if masked:
                kpos = lax.broadcasted_iota(jnp.int32, (blk, blk), 0)
                qpos = lax.broadcasted_iota(jnp.int32, (blk, blk), 1)
                t = jnp.where(kpos <= qpos, t, NEG_BIG)
            m_new = jnp.maximum(m_run, jnp.max(t, axis=0, keepdims=True) + cq[hh])
            alpha = jnp.exp2(m_run - m_new)
            pr = jnp.exp2(t - (m_new - cq[hh])).astype(BF16)
            v_rows = vt_ref[0, 0, hh * HEAD_DIM:(hh + 1) * HEAD_DIM, pl.ds(off, blk)]
            new.append((m_new, alpha * acc + _dot(jnp.concatenate([v_rows, ones], axis=0), pr)))
        return tuple(new)

    init = tuple((jnp.full((1, blk), NEG_BIG, F32), jnp.zeros((HEAD_DIM + DENOM_ROWS, blk), F32))
                 for _ in range(2))
    carry = lax.fori_loop(0, qi, lambda j, cr: step(j, cr, False), init)
    carry = step(qi, carry, True)
    out_t = jnp.concatenate([acc[0:HEAD_DIM] / acc[HEAD_DIM:HEAD_DIM + 1] for _, acc in carry], axis=0)
    o_ref[...] = out_t.T.astype(BF16)


def _attn_prompt(qt, k2, kc, vt, crow, bsz, seq, blk):
    n = bsz * seq
    nq = seq // blk
    crow4 = crow.reshape(bsz, HEAD_PAIRS, 2, seq)
    return pl.pallas_call(
        functools.partial(_attn_kernel, blk=blk),
        grid=(bsz, HEAD_PAIRS, nq),
        in_specs=[
            pl.BlockSpec((1, 1, LANES, blk), lambda b, hp, qi: (b, hp, 0, qi)),
            pl.BlockSpec((seq, LANES), lambda b, hp, qi: (b, hp)),
            pl.BlockSpec((seq, LANES), lambda b, hp, qi: (b, 0)),
            pl.BlockSpec((1, 1, LANES, seq), lambda b, hp, qi: (b, hp, 0, 0)),
            pl.BlockSpec((1, 1, 2, blk), lambda b, hp, qi: (b, hp, 0, qi)),
        ],
        out_specs=pl.BlockSpec((blk, LANES), lambda b, hp, qi: (b * nq + qi, hp)),
        out_shape=jax.ShapeDtypeStruct((n, C_WIDTH), BF16),
        compiler_params=pltpu.CompilerParams(
            dimension_semantics=("arbitrary", "arbitrary", "arbitrary"), vmem_limit_bytes=VMEM_LIMIT),
        name="forget_attn",
    )(qt, k2, kc, vt, crow4)


def _out_kernel(x_ref, a_ref, b_ref, c_ref, wout_ref, gpost_ref, o_ref):
    y = (_dot(a_ref[...], wout_ref[0:A_WIDTH, :])
         + _dot(b_ref[...], wout_ref[A_WIDTH:A_WIDTH + B_WIDTH, :])
         + _dot(c_ref[...], wout_ref[A_WIDTH + B_WIDTH:, :]))
    o_ref[...] = x_ref[...] + _rms(y, gpost_ref[...])


def _mix_out(x, a, b, c, wout, gpost, tm):
    n, d = x.shape
    row = lambda w: pl.BlockSpec((tm, w), lambda i: (i, 0))
    return pl.pallas_call(
        _out_kernel,
        grid=(n // tm,),
        in_specs=[row(d), row(A_WIDTH), row(B_WIDTH), row(C_WIDTH), _resident(wout.shape),
                  _resident((1, d))],
        out_specs=row(d),
        out_shape=jax.ShapeDtypeStruct((n, d), F32),
        compiler_params=pltpu.CompilerParams(dimension_semantics=("parallel",),
                                             vmem_limit_bytes=VMEM_LIMIT),
        name="mix_out",
    )(x, a, b, c, wout, gpost)


def _sproj_kernel(x_ref, gpre_ref, win_ref, wfl_ref, fb_ref, cvg_ref, wexp_ref, bsamp_ref, pbd_ref,
                  pscale_ref, state_ref,
                  a_ref, b_ref, q_ref, k_ref, v_ref, logf_ref, cnew_ref, vn_ref, pin_ref,
                  *, db, n_new, past):
    rows = db * n_new
    hb = _rms(x_ref[...], gpre_ref[...]).astype(BF16)
    p = _dot(hb, win_ref[...])
    fl = _dot(hb, wfl_ref[...])
    u = p[:, 0:256]
    va = p[:, 256:512]
    pin = p[:, 512:768]
    q_ref[...] = p[:, 768:1280]
    k_ref[...] = p[:, 1280:1792]
    v_ref[...] = p[:, 1792:2304]
    pin_ref[...] = pin

    lf = _log_sigmoid(fl + fb_ref[...])
    logf_ref[...] = lf[:, 0:C_HEADS]
    run = lf[0:db]
    cnew_ref[0:db, :] = run[:, 0:C_HEADS]
    for t in range(1, n_new):
        run = run + lf[t * db:(t + 1) * db]
        cnew_ref[t * db:(t + 1) * db, :] = run[:, 0:C_HEADS]

    vn = _head_rms(va, cvg_ref[...])
    vn_ref[...] = vn
    rr = lax.broadcasted_iota(jnp.int32, (rows, rows), 0)
    cc = lax.broadcasted_iota(jnp.int32, (rows, rows), 1)
    keep = ((rr % db) == (cc % db)) & ((cc // db) <= (rr // db))
    w_heads = [jnp.where(keep, wexp_ref[h], 0.0).astype(BF16) for h in range(A_HEADS)]
    z = _gate_rows(w_heads, vn.astype(BF16), bsamp_ref[...])
    a_ref[...] = (u * z).astype(BF16)

    def ext(idx, ls):
        if idx < POOL_BUF:
            return state_ref[idx][:, ls]
        t = idx - POOL_BUF
        return pin[t * db:(t + 1) * db, ls]

    d_rows = []
    for t in range(n_new):
        d_halves = []
        for half in range(2):
            ls = slice(half * LANES, (half + 1) * LANES)
            d_halves.append(_pool_delta(lambda j: ext(POOL_BUF + t - j, ls),
                                        lambda w: float(min(past + t + 1, w)), half))
        d_rows.append(jnp.concatenate(d_halves, axis=1))
    d = jnp.concatenate(d_rows, axis=0).astype(BF16)
    b_ref[...] = (_dot(d, pbd_ref[...]) * pscale_ref[...]).astype(BF16)


def _proj_sample(x, gpre, win, wfl, fb, cvg, wexp, bsamp, pbd, pscale, state_t, db, n_new, past):
    rows, d = x.shape
    out_shapes = (
        jax.ShapeDtypeStruct((rows, A_WIDTH), BF16),
        jax.ShapeDtypeStruct((rows, B_WIDTH), BF16),
        jax.ShapeDtypeStruct((rows, C_WIDTH), F32),
        jax.ShapeDtypeStruct((rows, C_WIDTH), F32),
        jax.ShapeDtypeStruct((rows, C_WIDTH), F32),
        jax.ShapeDtypeStruct((rows, C_HEADS), F32),
        jax.ShapeDtypeStruct((rows, C_HEADS), F32),
        jax.ShapeDtypeStruct((rows, A_WIDTH), F32),
        jax.ShapeDtypeStruct((rows, B_WIDTH), F32),
    )
    args = (x, gpre, win, wfl, fb, cvg, wexp, bsamp, pbd, pscale, state_t)
    return pl.pallas_call(
        functools.partial(_sproj_kernel, db=db, n_new=n_new, past=past),
        grid=(1,),
        in_specs=[pl.BlockSpec(a.shape, lambda i, nd=a.ndim: (0,) * nd) for a in args],
        out_specs=tuple(pl.BlockSpec(s.shape, lambda i: (0, 0)) for s in out_shapes),
        out_shape=out_shapes,
        compiler_params=pltpu.CompilerParams(dimension_semantics=("arbitrary",),
                                             vmem_limit_bytes=VMEM_LIMIT),
        name="proj_mix_sample",
    )(*args)


def _pattn_kernel(pt_ref, q_ref, knew_ref, vnew_ref, cq_ref, cknew_ref, *rest, group, n_new):
    k_refs = rest[:group]
    v_refs = rest[group:2 * group]
    lf_refs = rest[2 * group:3 * group]
    o_ref, qbd_ref, m_ref, l_ref, acc_ref, carry_ref = rest[3 * group:]
    g = pl.program_id(1)
    rows = n_new * C_HEADS
    row_head = lax.broadcasted_iota(jnp.int32, (rows, C_WIDTH), 0) % C_HEADS
    lane_head = lax.broadcasted_iota(jnp.int32, (rows, C_WIDTH), 1) // HEAD_DIM
    own = row_head == lane_head

    @pl.when(g == 0)
    def _():
        qs = q_ref[0] * (HEAD_DIM ** -0.5)
        qrep = jnp.concatenate([jnp.broadcast_to(qs[t:t + 1], (C_HEADS, C_WIDTH)) for t in range(n_new)],
                               axis=0)
        qbd_ref[...] = jnp.where(own, qrep, 0.0).astype(BF16)
        m_ref[...] = jnp.full_like(m_ref, NEG_BIG)
        l_ref[...] = jnp.zeros_like(l_ref)
        acc_ref[...] = jnp.zeros_like(acc_ref)
        carry_ref[...] = jnp.zeros_like(carry_ref)

    qbd = qbd_ref[...]
    cq = cq_ref[0]

    def update(scores, pv):
        m_run = m_ref[...]
        m_cur = m_run
        for s in scores:
            m_cur = jnp.maximum(m_cur, jnp.max(s, axis=1, keepdims=True))
        alpha = jnp.exp(m_run - m_cur)
        l_new = alpha * l_ref[...]
        acc = alpha[:, 0:1] * acc_ref[...]
        for i, s in enumerate(scores):
            pr = jnp.exp(s - m_cur[:, 0:1])
            l_new = l_new + jnp.sum(pr, axis=1, keepdims=True)
            acc = acc + pv(i, pr.astype(BF16))
        m_ref[...] = m_cur
        l_ref[...] = l_new
        acc_ref[...] = acc

    page = lf_refs[0].shape[-1]
    sr = lax.broadcasted_iota(jnp.int32, (page, 2 * page), 0)
    sc = lax.broadcasted_iota(jnp.int32, (page, 2 * page), 1)
    later_or_all = jnp.where((sr > sc) | (sc >= page), 1.0, 0.0).astype(BF16)
    sums = _exact_rows_dot(jnp.concatenate([lf_refs[i][0, 0] for i in range(group)], axis=0), later_or_all)
    carry = carry_ref[...]
    scores = []
    for i in range(group):
        hs = slice(i * C_HEADS, (i + 1) * C_HEADS)
        bias = sums[hs, 0:page] + carry
        carry = carry + sums[hs, page:2 * page]
        s = _dot(qbd, k_refs[i][0, 0].astype(BF16))
        scores.append(s + jnp.concatenate([bias] * n_new, axis=0) + cq)
    carry_ref[...] = carry
    update(scores, lambda i, pr: _dot_nt(pr, v_refs[i][0, 0].astype(BF16)))

    @pl.when(g == pl.num_programs(1) - 1)
    def _():
        s = _dot_nt(qbd, knew_ref[0].astype(BF16))
        s = s + cq[:, 0:NEW_PAD] - cknew_ref[0]
        tq = lax.broadcasted_iota(jnp.int32, (rows, NEW_PAD), 0) // C_HEADS
        sk = lax.broadcasted_iota(jnp.int32, (rows, NEW_PAD), 1)
        s = jnp.where(sk <= tq, s, NEG_BIG)
        update([s], lambda i, pr: _dot(pr, vnew_ref[0].astype(BF16)))
        full = jnp.where(own, acc_ref[...] / l_ref[...][:, 0:1], 0.0)
        o_ref[0] = jnp.sum(full.reshape(n_new, C_HEADS, C_WIDTH), axis=1).astype(BF16)


def _paged_attn(page_table, q_b, knew_b, vnew_b, cq_rep, cknew, cache_kt, cache_vt, cache_lft, layer, group):
    db, n_pages = page_table.shape
    n_new = q_b.shape[1]
    page = cache_kt.shape[-1]
    rows = n_new * C_HEADS
    per_seq = lambda shape: pl.BlockSpec((1,) + shape, lambda b, g, pt: (b,) + (0,) * len(shape))

    def page_spec(i, height):
        def index(b, g, pt):
            return (layer, pt[b, n_pages - 1 - (g * group + i)], 0, 0)
        return pl.BlockSpec((1, 1, height, page), index)

    in_specs = [per_seq((n_new, C_WIDTH)), per_seq(knew_b.shape[1:]), per_seq(vnew_b.shape[1:]),
                per_seq((rows, LANES)), per_seq(cknew.shape[1:])]
    in_specs += [page_spec(i, C_WIDTH) for i in range(group)] * 2
    in_specs += [page_spec(i, C_HEADS) for i in range(group)]
    grid_spec = pltpu.PrefetchScalarGridSpec(
        num_scalar_prefetch=1,
        grid=(db, n_pages // group),
        in_specs=in_specs,
        out_specs=pl.BlockSpec((1, n_new, C_WIDTH), lambda b, g, pt: (b, 0, 0)),
        scratch_shapes=[pltpu.VMEM((rows, C_WIDTH), BF16), pltpu.VMEM((rows, LANES), F32),
                        pltpu.VMEM((rows, LANES), F32), pltpu.VMEM((rows, C_WIDTH), F32),
                        pltpu.VMEM((C_HEADS, page), F32)],
    )
    return pl.pallas_call(
        functools.partial(_pattn_kernel, group=group, n_new=n_new),
        grid_spec=grid_spec,
        out_shape=jax.ShapeDtypeStruct((db, n_new, C_WIDTH), BF16),
        compiler_params=pltpu.CompilerParams(dimension_semantics=("arbitrary", "arbitrary"),
                                             vmem_limit_bytes=VMEM_LIMIT),
        name="paged_forget_attn",
    )(page_table, q_b, knew_b, vnew_b, cq_rep, cknew,
      *([cache_kt] * group), *([cache_vt] * group), *([cache_lft] * group))


def _row_tile(n, target):
    t = min(n, target)
    while n % t:
        t //= 2
    return t


def kernel(x_prompt, x_sample, cache_k, cache_v, cache_logf, state_pool, page_table,
           ffn1_g_pre, ffn1_g_post, ffn1_w_gate, ffn1_w_up, ffn1_w_down,
           mix_g_pre, mix_g_post, w_in, w_out,
           chunk_v_g, chunk_w_s, chunk_b, pool_w, pool_scale, forget_b,
           ffn2_g_pre, ffn2_g_post, ffn2_w_gate, ffn2_w_up, ffn2_w_down):
    bsz, seq, d = x_prompt.shape
    db, n_new, _ = x_sample.shape
    depth, n_phys, page = cache_k.shape[:3]
    n_pages = page_table.shape[1]
    past = n_pages * page
    n = bsz * seq
    rows_s = db * n_new
    d_main = w_in.shape[-1] - C_HEADS

    tm_ffn = _row_tile(n, 512)
    tm_proj = _row_tile(seq, 512)
    blk = _row_tile(seq, 1024)
    group = _row_tile(n_pages, 16)

    xp = x_prompt.reshape(n, d)
    xs = jnp.transpose(x_sample, (1, 0, 2)).reshape(rows_s, d)
    cache_kt = jnp.transpose(cache_k, (0, 1, 3, 4, 2)).reshape(depth, n_phys, C_WIDTH, page)
    cache_vt = jnp.transpose(cache_v, (0, 1, 3, 4, 2)).reshape(depth, n_phys, C_WIDTH, page)
    cache_lft = jnp.transpose(cache_logf, (0, 1, 3, 2))

    outs = {k: [] for k in ("kp", "vp", "lfp", "poolp", "ks", "vs", "lfs", "pools", "cvs")}
    for l in range(depth):
        row2 = lambda v: v[l].reshape(1, -1)
        f1 = (row2(ffn1_g_pre), row2(ffn1_g_post), ffn1_w_gate[l].astype(BF16),
              ffn1_w_up[l].astype(BF16), ffn1_w_down[l].astype(BF16))
        f2 = (row2(ffn2_g_pre), row2(ffn2_g_post), ffn2_w_gate[l].astype(BF16),
              ffn2_w_up[l].astype(BF16), ffn2_w_down[l].astype(BF16))
        win = w_in[l, :, :d_main].astype(BF16)
        wfl = jnp.pad(w_in[l, :, d_main:], ((0, 0), (0, LANES - C_HEADS))).astype(BF16)
        fb = jnp.pad(forget_b[l], (0, LANES - C_HEADS)).reshape(1, LANES)
        wout = w_out[l].astype(BF16)
        bmat = jnp.repeat(chunk_b[l].T, HEAD_DIM, axis=1)
        pbd = jax.scipy.linalg.block_diag(*[pool_w[l, g] for g in range(len(POOL_WINDOWS))]).astype(BF16)
        mixp = (row2(mix_g_pre), win, wfl, fb, row2(chunk_v_g))
        pool_p = (pbd, row2(pool_scale))

        xp = _ffn(xp, *f1, tm_ffn)
        (a, b, qt, k2, kc, vt, kf_t, vf_t, lf_t, crow, tail) = _proj_prompt(
            xp, bsz, seq, *mixp, chunk_w_s[l], bmat, *pool_p, tm_proj)
        c = _attn_prompt(qt, k2, kc, vt, crow, bsz, seq, blk)
        xp = _mix_out(xp, a, b, c, wout, row2(mix_g_post), tm_ffn)
        xp = _ffn(xp, *f2, tm_ffn)
        to_heads = lambda v_t: jnp.transpose(v_t.reshape(bsz, C_HEADS, HEAD_DIM, seq), (0, 3, 1, 2))
        outs["kp"].append(to_heads(kf_t))
        outs["vp"].append(to_heads(vf_t))
        outs["lfp"].append(jnp.transpose(lf_t, (0, 2, 1)))
        outs["poolp"].append(tail[:, POOL_CARRY - POOL_BUF:])

        xs = _ffn(xs, *f1, rows_s)
        wexp = jnp.repeat(jnp.repeat(chunk_w_s[l, :, :n_new, :n_new], db, axis=1), db, axis=2)
        bsamp = jnp.repeat(bmat[:n_new], db, axis=0)
        state_t = jnp.transpose(state_pool[l], (1, 0, 2))
        (a_s, b_s, q_s, k_s, v_s, lf_s, cnew, vn_s, pin_s) = _proj_sample(
            xs, *mixp, wexp, bsamp, *pool_p, state_t, db, n_new, past)
        to_b = lambda v: jnp.transpose(v.reshape(n_new, db, -1), (1, 0, 2))
        q_b, k_b, v_b, cnew_b = to_b(q_s), to_b(k_s), to_b(v_s), to_b(cnew)
        pad_new = ((0, 0), (0, NEW_PAD - n_new), (0, 0))
        cq_rep = jnp.broadcast_to(cnew_b.reshape(db, n_new * C_HEADS, 1), (db, n_new * C_HEADS, LANES))
        cknew = jnp.pad(jnp.tile(jnp.transpose(cnew_b, (0, 2, 1)), (1, n_new, 1)),
                        ((0, 0), (0, 0), (0, NEW_PAD - n_new)))
        c_s = _paged_attn(page_table, q_b, jnp.pad(k_b, pad_new), jnp.pad(v_b, pad_new), cq_rep, cknew,
                          cache_kt, cache_vt, cache_lft, l, group)
        c_s = jnp.transpose(c_s, (1, 0, 2)).reshape(rows_s, C_WIDTH)
        xs = _mix_out(xs, a_s, b_s, c_s, wout, row2(mix_g_post), rows_s)
        xs = _ffn(xs, *f2, rows_s)
        outs["ks"].append(k_b.reshape(db, n_new, C_HEADS, HEAD_DIM))
        outs["vs"].append(v_b.reshape(db, n_new, C_HEADS, HEAD_DIM))
        outs["lfs"].append(to_b(lf_s))
        outs["pools"].append(jnp.concatenate([state_pool[l], to_b(pin_s)], axis=1)[:, -POOL_BUF:])
        outs["cvs"].append(to_b(vn_s))

    y_prompt = xp.reshape(bsz, seq, d)
    y_sample = jnp.transpose(xs.reshape(n_new, db, d), (1, 0, 2))
    st = lambda key: jnp.stack(outs[key])
    return (y_prompt, y_sample, st("kp"), st("vp"), st("lfp"), st("poolp"),
            st("ks"), st("vs"), st("lfs"), st("pools"), st("cvs"))
```

```python
import functools

import jax
import jax.numpy as jnp
from jax import lax
from jax.experimental import pallas as pl
from jax.experimental.pallas import tpu as pltpu

F32 = jnp.float32
BF16 = jnp.bfloat16

RMS_EPS = 1e-6
FFN_RES = 0.5
HEAD_DIM = 64
CHUNK = 128
POOL_WINDOWS = (2, 4, 8, 16)
POOL_BUF = max(POOL_WINDOWS) - 1
POOL_CARRY = POOL_BUF + 1
A_WIDTH = 256
A_HEADS = A_WIDTH // HEAD_DIM
B_WIDTH = 256
C_HEADS = 8
C_WIDTH = C_HEADS * HEAD_DIM
LANES = 128
HEAD_PAIRS = C_WIDTH // LANES
NEW_PAD = 16
DENOM_ROWS = 16
LOG2E = 1.4426950408889634
STALE_MAX_SLACK = 16.0
NEG_BIG = -1e30
VMEM_LIMIT = 56 * 1024 * 1024


def _resident(shape):
    nd = len(shape)
    return pl.BlockSpec(shape, lambda *_: (0,) * nd, pipeline_mode=pl.Buffered(1))


def _rms(x, g):
    return x * lax.rsqrt(jnp.mean(x * x, axis=-1, keepdims=True) + RMS_EPS) * g


def _split3(x):
    hi = x.astype(BF16)
    r = x - hi.astype(F32)
    mid = r.astype(BF16)
    lo = (r - mid.astype(F32)).astype(BF16)
    return hi, mid, lo


def _dot(a, b):
    return jnp.dot(a, b, preferred_element_type=F32)


def _dot_nt(a, b):
    return lax.dot_general(a, b, (((1,), (1,)), ((), ())), preferred_element_type=F32)


def _exact_rows_dot(x, ones_mat):
    rows = x.shape[0]
    hi, mid, lo = _split3(x)
    parts = jnp.concatenate([hi.astype(F32), mid.astype(F32), lo.astype(F32)], axis=0).astype(BF16)
    out = _dot(parts, ones_mat)
    return out[0:rows] + out[rows:2 * rows] + out[2 * rows:3 * rows]


def _half_ffn(x, gpre_ref, gpost_ref, wg_ref, wu_ref, wd_ref, f_chunks):
    hb = _rms(x, gpre_ref[...]).astype(BF16)
    f = None
    for c0, c1 in f_chunks:
        g = _dot(hb, wg_ref[:, c0:c1])
        u = _dot(hb, wu_ref[:, c0:c1])
        act = (g / (1.0 + jnp.exp(-g)) * u).astype(BF16)
        part = _dot(act, wd_ref[c0:c1, :])
        f = part if f is None else f + part
    return x + FFN_RES * _rms(f, gpost_ref[...])


def _ffn_kernel(x_ref, gpre_ref, gpost_ref, wg_ref, wu_ref, wd_ref, o_ref, *, f_chunks):
    o_ref[...] = _half_ffn(x_ref[...], gpre_ref, gpost_ref, wg_ref, wu_ref, wd_ref, f_chunks)


def _mix_ffn_kernel(x_ref, a_ref, b_ref, c_ref, wout_ref, gmix_ref, gpre_ref, gpost_ref, wg_ref, wu_ref,
                    wd_ref, o_ref, *, f_chunks):
    y = (_dot(a_ref[...], wout_ref[0:A_WIDTH, :])
         + _dot(b_ref[...], wout_ref[A_WIDTH:A_WIDTH + B_WIDTH, :])
         + _dot(c_ref[...], wout_ref[A_WIDTH + B_WIDTH:, :]))
    x = x_ref[...] + _rms(y, gmix_ref[...])
    o_ref[...] = _half_ffn(x, gpre_ref, gpost_ref, wg_ref, wu_ref, wd_ref, f_chunks)


def _ffn_chunks(f):
    half = (f // 2 + 255) // 256 * 256
    return ((0, half), (half, f)) if half < f else ((0, f),)


def _ffn(x, gpre, gpost, wg, wu, wd, tm):
    n, d = x.shape
    f = wg.shape[1]
    row = pl.BlockSpec((tm, d), lambda i: (i, 0))
    return pl.pallas_call(
        functools.partial(_ffn_kernel, f_chunks=_ffn_chunks(f)),
        grid=(n // tm,),
        in_specs=[row, _resident((1, d)), _resident((1, d)), _resident((d, f)), _resident((d, f)),
                  _resident((f, d))],
        out_specs=row,
        out_shape=jax.ShapeDtypeStruct((n, d), F32),
        compiler_params=pltpu.CompilerParams(dimension_semantics=("parallel",),
                                             vmem_limit_bytes=VMEM_LIMIT),
        name="half_ffn",
    )(x, gpre, gpost, wg, wu, wd)


def _mix_ffn(x, a, b, c, wout, gmix, gpre, gpost, wg, wu, wd, tm):
    n, d = x.shape
    f = wg.shape[1]
    row = lambda w: pl.BlockSpec((tm, w), lambda i: (i, 0))
    return pl.pallas_call(
        functools.partial(_mix_ffn_kernel, f_chunks=_ffn_chunks(f)),
        grid=(n // tm,),
        in_specs=[row(d), row(A_WIDTH), row(B_WIDTH), row(C_WIDTH), _resident(wout.shape),
                  _resident((1, d)), _resident((1, d)), _resident((1, d)), _resident((d, f)),
                  _resident((d, f)), _resident((f, d))],
        out_specs=row(d),
        out_shape=jax.ShapeDtypeStruct((n, d), F32),
        compiler_params=pltpu.CompilerParams(dimension_semantics=("parallel",),
                                             vmem_limit_bytes=VMEM_LIMIT),
        name="mix_out_ffn",
    )(x, a, b, c, wout, gmix, gpre, gpost, wg, wu, wd)


def _log_sigmoid(z):
    return jnp.minimum(z, 0.0) - jnp.log1p(jnp.exp(-jnp.abs(z)))


def _head_rms(va, cvg):
    r = lax.broadcasted_iota(jnp.int32, (A_WIDTH, A_WIDTH), 0) // HEAD_DIM
    c = lax.broadcasted_iota(jnp.int32, (A_WIDTH, A_WIDTH), 1) // HEAD_DIM
    same_head = jnp.where(r == c, 1.0, 0.0).astype(BF16)
    x2 = va * va
    x2h = x2.astype(BF16)
    x2l = (x2 - x2h.astype(F32)).astype(BF16)
    ms = (_dot(x2h, same_head) + _dot(x2l, same_head)) * (1.0 / HEAD_DIM)
    return va * lax.rsqrt(ms + RMS_EPS) * cvg


def _gate_rows(w_heads, vnb, bias):
    rows = vnb.shape[0]
    lane_head = lax.broadcasted_iota(jnp.int32, (rows, A_WIDTH), 1) // HEAD_DIM
    z = bias
    for h in range(A_HEADS):
        z = z + jnp.where(lane_head == h, _dot(w_heads[h], vnb), 0.0)
    return z


def _pool_delta(delayed, count, half):
    w_small, w_big = POOL_WINDOWS[2 * half], POOL_WINDOWS[2 * half + 1]
    cur = delayed(0)
    acc = cur
    for j in range(1, w_small):
        acc = acc + delayed(j)
    acc_small = acc
    for j in range(w_small, w_big):
        acc = acc + delayed(j)
    is_small = lax.broadcasted_iota(jnp.int32, cur.shape, 1) < HEAD_DIM
    mean = jnp.where(is_small, acc_small, acc) / jnp.where(is_small, count(w_small), count(w_big))
    return mean - cur


def _proj_kernel(x_ref, gpre_ref, win_ref, wfl_ref, fb_ref, cvg_ref, ws_ref, bmat_ref, pbd_ref,
                 pscale_ref, k_all_ref, v_all_ref,
                 a_ref, b_ref, qt_ref, k2_ref, kc_ref, vt_ref, kout_ref, vout_ref, logf_ref, crow_ref,
                 tail_ref, ext_ref, carry_ref, *, tm):
    del k_all_ref, v_all_ref
    i = pl.program_id(1)
    x = x_ref[...]
    hb = _rms(x, gpre_ref[...]).astype(BF16)
    p = _dot(hb, win_ref[...])
    fl = _dot(hb, wfl_ref[...])
    u = p[:, 0:256]
    va = p[:, 256:512]
    pin = p[:, 512:768]
    q = p[:, 768:1280]
    k = p[:, 1280:1792]
    v = p[:, 1792:2304]

    k2_ref[...] = k.astype(BF16)
    for hp in range(HEAD_PAIRS):
        sl = slice(hp * LANES, (hp + 1) * LANES)
        qt_ref[0, hp] = (q[:, sl] * (HEAD_DIM ** -0.5 * LOG2E)).T.astype(BF16)
        kout_ref[0, 0, hp] = k[:, sl].T
        v_t = v[:, sl].T
        vout_ref[0, 0, hp] = v_t
        vt_ref[0, hp] = v_t.astype(BF16)

    lf_t = _log_sigmoid(fl + fb_ref[...]).T[0:C_HEADS, :]
    logf_ref[0] = lf_t
    r = lax.broadcasted_iota(jnp.int32, (tm, tm), 0)
    c = lax.broadcasted_iota(jnp.int32, (tm, tm), 1)
    upto = jnp.where(r <= c, 1.0, 0.0).astype(BF16)

    @pl.when(i == 0)
    def _():
        carry_ref[...] = jnp.zeros_like(carry_ref)

    csum = _exact_rows_dot(lf_t, upto) + carry_ref[:, 0:1]
    carry_ref[...] = jnp.broadcast_to(csum[:, tm - 1:tm], carry_ref.shape)
    csum2 = csum * LOG2E
    crow_ref[0] = csum2
    hi, mid, lo = _split3(csum2)
    cols = jnp.concatenate([hi.astype(F32), mid.astype(F32), lo.astype(F32),
                            jnp.zeros((LANES - 3 * C_HEADS, tm), F32)], axis=0)
    kc_ref[...] = cols.T.astype(BF16)

    vn = _head_rms(va, cvg_ref[...])
    vnb = vn.astype(BF16)
    rr = lax.broadcasted_iota(jnp.int32, (CHUNK, CHUNK), 0)
    cc = lax.broadcasted_iota(jnp.int32, (CHUNK, CHUNK), 1)
    w_heads = [jnp.where(cc <= rr, ws_ref[h], 0.0).astype(BF16) for h in range(A_HEADS)]
    for ci in range(tm // CHUNK):
        rs = slice(ci * CHUNK, (ci + 1) * CHUNK)
        z = _gate_rows(w_heads, vnb[rs], bmat_ref[...])
        a_ref[rs, :] = (u[rs] * z).astype(BF16)

    @pl.when(i == 0)
    def _():
        ext_ref[0:POOL_CARRY, :] = jnp.zeros((POOL_CARRY, B_WIDTH), F32)

    ext_ref[POOL_CARRY:POOL_CARRY + tm, :] = pin
    pos = i * tm + lax.broadcasted_iota(jnp.int32, (tm, LANES), 0)
    d_halves = []
    for half in range(2):
        ls = slice(half * LANES, (half + 1) * LANES)
        d_halves.append(_pool_delta(
            lambda j: ext_ref[pl.ds(POOL_CARRY - j, tm), ls],
            lambda w: jnp.minimum(pos + 1, w).astype(F32), half))
    d = jnp.concatenate(d_halves, axis=1).astype(BF16)
    b_ref[...] = (_dot(d, pbd_ref[...]) * pscale_ref[...]).astype(BF16)
    tail = ext_ref[tm:tm + POOL_CARRY, :]
    tail_ref[0] = tail
    ext_ref[0:POOL_CARRY, :] = tail


def _proj_prompt(x, bsz, seq, gpre, win, wfl, fb, cvg, ws, bmat, pbd, pscale, k_all, v_all, layer, tm):
    n, d = x.shape
    nt = seq // tm
    dp = win.shape[1]
    row = lambda w: pl.BlockSpec((tm, w), lambda b, i: (b * nt + i, 0))
    pair_t = pl.BlockSpec((1, HEAD_PAIRS, LANES, tm), lambda b, i: (b, 0, 0, i))
    layer_pair_t = pl.BlockSpec((1, 1, HEAD_PAIRS, LANES, tm), lambda b, i: (layer, b, 0, 0, i))
    head_t = pl.BlockSpec((1, C_HEADS, tm), lambda b, i: (b, 0, i))
    pair_shape = (bsz, HEAD_PAIRS, LANES, seq)
    untouched = pl.BlockSpec(memory_space=pl.ANY)
    out_shapes = (
        jax.ShapeDtypeStruct((n, A_WIDTH), BF16),
        jax.ShapeDtypeStruct((n, B_WIDTH), BF16),
        jax.ShapeDtypeStruct(pair_shape, BF16),
        jax.ShapeDtypeStruct((n, C_WIDTH), BF16),
        jax.ShapeDtypeStruct((n, LANES), BF16),
        jax.ShapeDtypeStruct(pair_shape, BF16),
        jax.ShapeDtypeStruct(k_all.shape, F32),
        jax.ShapeDtypeStruct(v_all.shape, F32),
        jax.ShapeDtypeStruct((bsz, C_HEADS, seq), F32),
        jax.ShapeDtypeStruct((bsz, C_HEADS, seq), F32),
        jax.ShapeDtypeStruct((bsz, POOL_CARRY, B_WIDTH), F32),
    )
    out_specs = (
        row(A_WIDTH), row(B_WIDTH), pair_t, row(C_WIDTH), row(LANES), pair_t, layer_pair_t, layer_pair_t,
        head_t, head_t,
        pl.BlockSpec((1, POOL_CARRY, B_WIDTH), lambda b, i: (b, 0, 0)),
    )
    return pl.pallas_call(
        functools.partial(_proj_kernel, tm=tm),
        grid=(bsz, nt),
        in_specs=[row(d), _resident((1, d)), _resident((d, dp)), _resident((d, LANES)),
                  _resident((1, LANES)), _resident((1, A_WIDTH)), _resident((A_HEADS, CHUNK, CHUNK)),
                  _resident((CHUNK, A_WIDTH)), _resident((B_WIDTH, B_WIDTH)), _resident((1, B_WIDTH)),
                  untouched, untouched],
        out_specs=out_specs,
        out_shape=out_shapes,
        input_output_aliases={10: 6, 11: 7},
        scratch_shapes=[pltpu.VMEM((tm + POOL_CARRY, B_WIDTH), F32), pltpu.VMEM((C_HEADS, LANES), F32)],
        compiler_params=pltpu.CompilerParams(dimension_semantics=("arbitrary", "arbitrary"),
                                             vmem_limit_bytes=VMEM_LIMIT),
        name="proj_mix",
    )(x, gpre, win, wfl, fb, cvg, ws, bmat, pbd, pscale, k_all, v_all)


def _attn_kernel(qt_ref, k2_ref, kc_ref, vt_ref, cq_ref, o_ref, *, blk):
    hp = pl.program_id(1)
    qi = pl.program_id(2)
    q2 = qt_ref[0, 0]
    row = lax.broadcasted_iota(jnp.int32, (LANES, blk), 0)
    qa, cq = [], []
    for hh in range(2):
        own_q = jnp.where((row // HEAD_DIM) == hh, q2, jnp.zeros_like(q2))
        minus_ck = jnp.where((row < 3 * C_HEADS) & ((row % C_HEADS) == 2 * hp + hh), -1.0, 0.0)
        qa.append(jnp.concatenate([own_q, minus_ck.astype(BF16)], axis=0))
        cq.append(cq_ref[0, 0, hh:hh + 1, :])
    ones = jnp.ones((DENOM_ROWS, blk), BF16)

    def load(j):
        off = pl.multiple_of(j * blk, blk)
        kk = jnp.concatenate([k2_ref[pl.ds(off, blk), :], kc_ref[pl.ds(off, blk), :]], axis=1)
        va = [jnp.concatenate([vt_ref[0, 0, hh * HEAD_DIM:(hh + 1) * HEAD_DIM, pl.ds(off, blk)], ones], axis=0)
              for hh in range(2)]
        return kk, va

    def exact(j, carry, masked):
        kk, va = load(j)
        ts = [_dot(kk, qa[hh]) for hh in range(2)]
        new = []
        for hh in range(2):
            m_run, acc = carry[hh]
            t = ts[hh]
            if masked:
                kpos = lax.broadcasted_iota(jnp.int32, (blk, blk), 0)
                qpos = lax.broadcasted_iota(jnp.int32, (blk, blk), 1)
                t = jnp.where(kpos <= qpos, t, NEG_BIG)
            m_new = jnp.maximum(m_run, jnp.max(t, axis=0, keepdims=True) + cq[hh])
            pr = jnp.exp2(t - (m_new - cq[hh])).astype(BF16)
            new.append((m_new, jnp.exp2(m_run - m_new) * acc + _dot(va[hh], pr)))
        return tuple(new)

    init = tuple((jnp.full((1, blk), NEG_BIG, F32), jnp.zeros((HEAD_DIM + DENOM_ROWS, blk), F32))
                 for _ in range(2))
    carry = exact(qi, init, True)

    def older(i, carry):
        j = qi - 1 - i
        kk, va = load(j)
        new, rise = [], []
        for hh in range(2):
            m_run, acc = carry[hh]
            t = _dot(kk, qa[hh])
            pr = jnp.exp2(t - (m_run - cq[hh])).astype(BF16)
            m_new = jnp.maximum(m_run, jnp.max(t, axis=0, keepdims=True) + cq[hh])
            new.append((m_new, jnp.exp2(m_run - m_new) * (acc + _dot(va[hh], pr))))
            rise.append(m_new - m_run)
        ok = jnp.all(jnp.concatenate(rise, axis=1) <= STALE_MAX_SLACK)
        return lax.cond(ok, lambda: tuple(new), lambda: exact(j, carry, False))

    carry = lax.fori_loop(0, qi, older, carry)
    out_t = jnp.concatenate([acc[0:HEAD_DIM] / acc[HEAD_DIM:HEAD_DIM + 1] for _, acc in carry], axis=0)
    o_ref[...] = out_t.T.astype(BF16)


def _attn_prompt(qt, k2, kc, vt, crow, bsz, seq, blk):
    n = bsz * seq
    nq = seq // blk
    crow4 = crow.reshape(bsz, HEAD_PAIRS, 2, seq)
    return pl.pallas_call(
        functools.partial(_attn_kernel, blk=blk),
        grid=(bsz, HEAD_PAIRS, nq),
        in_specs=[
            pl.BlockSpec((1, 1, LANES, blk), lambda b, hp, qi: (b, hp, 0, qi)),
            pl.BlockSpec((seq, LANES), lambda b, hp, qi: (b, hp)),
            pl.BlockSpec((seq, LANES), lambda b, hp, qi: (b, 0)),
            pl.BlockSpec((1, 1, LANES, seq), lambda b, hp, qi: (b, hp, 0, 0)),
            pl.BlockSpec((1, 1, 2, blk), lambda b, hp, qi: (b, hp, 0, qi)),
        ],
        out_specs=pl.BlockSpec((blk, LANES), lambda b, hp, qi: (b * nq + qi, hp)),
        out_shape=jax.ShapeDtypeStruct((n, C_WIDTH), BF16),
        compiler_params=pltpu.CompilerParams(
            dimension_semantics=("arbitrary", "arbitrary", "arbitrary"), vmem_limit_bytes=VMEM_LIMIT),
        name="forget_attn",
    )(qt, k2, kc, vt, crow4)


def _sproj_kernel(x_ref, gpre_ref, win_ref, wfl_ref, fb_ref, cvg_ref, wexp_ref, bsamp_ref, pbd_ref,
                  pscale_ref, state_ref,
                  a_ref, b_ref, q_ref, k_ref, v_ref, logf_ref, cnew_ref, vn_ref, pin_ref,
                  *, db, n_new, past):
    rows = db * n_new
    hb = _rms(x_ref[...], gpre_ref[...]).astype(BF16)
    p = _dot(hb, win_ref[...])
    fl = _dot(hb, wfl_ref[...])
    u = p[:, 0:256]
    va = p[:, 256:512]
    pin = p[:, 512:768]
    q_ref[...] = p[:, 768:1280]
    k_ref[...] = p[:, 1280:1792]
    v_ref[...] = p[:, 1792:2304]
    pin_ref[...] = pin

    lf = _log_sigmoid(fl + fb_ref[...])
    logf_ref[...] = lf[:, 0:C_HEADS]
    run = lf[0:db]
    cnew_ref[0:db, :] = run[:, 0:C_HEADS]
    for t in range(1, n_new):
        run = run + lf[t * db:(t + 1) * db]
        cnew_ref[t * db:(t + 1) * db, :] = run[:, 0:C_HEADS]

    vn = _head_rms(va, cvg_ref[...])
    vn_ref[...] = vn
    rr = lax.broadcasted_iota(jnp.int32, (rows, rows), 0)
    cc = lax.broadcasted_iota(jnp.int32, (rows, rows), 1)
    keep = ((rr % db) == (cc % db)) & ((cc // db) <= (rr // db))
    w_heads = [jnp.where(keep, wexp_ref[h], 0.0).astype(BF16) for h in range(A_HEADS)]
    z = _gate_rows(w_heads, vn.astype(BF16), bsamp_ref[...])
    a_ref[...] = (u * z).astype(BF16)

    def ext(idx, ls):
        if idx < POOL_BUF:
            return state_ref[idx][:, ls]
        t = idx - POOL_BUF
        return pin[t * db:(t + 1) * db, ls]

    d_rows = []
    for t in range(n_new):
        d_halves = []
        for half in range(2):
            ls = slice(half * LANES, (half + 1) * LANES)
            d_halves.append(_pool_delta(lambda j: ext(POOL_BUF + t - j, ls),
                                        lambda w: float(min(past + t + 1, w)), half))
        d_rows.append(jnp.concatenate(d_halves, axis=1))
    d = jnp.concatenate(d_rows, axis=0).astype(BF16)
    b_ref[...] = (_dot(d, pbd_ref[...]) * pscale_ref[...]).astype(BF16)


def _proj_sample(x, gpre, win, wfl, fb, cvg, wexp, bsamp, pbd, pscale, state_t, db, n_new, past):
    rows, d = x.shape
    out_shapes = (
        jax.ShapeDtypeStruct((rows, A_WIDTH), BF16),
        jax.ShapeDtypeStruct((rows, B_WIDTH), BF16),
        jax.ShapeDtypeStruct((rows, C_WIDTH), F32),
        jax.ShapeDtypeStruct((rows, C_WIDTH), F32),
        jax.ShapeDtypeStruct((rows, C_WIDTH), F32),
        jax.ShapeDtypeStruct((rows, C_HEADS), F32),
        jax.ShapeDtypeStruct((rows, C_HEADS), F32),
        jax.ShapeDtypeStruct((rows, A_WIDTH), F32),
        jax.ShapeDtypeStruct((rows, B_WIDTH), F32),
    )
    args = (x, gpre, win, wfl, fb, cvg, wexp, bsamp, pbd, pscale, state_t)
    return pl.pallas_call(
        functools.partial(_sproj_kernel, db=db, n_new=n_new, past=past),
        grid=(1,),
        in_specs=[pl.BlockSpec(a.shape, lambda i, nd=a.ndim: (0,) * nd) for a in args],
        out_specs=tuple(pl.BlockSpec(s.shape, lambda i: (0, 0)) for s in out_shapes),
        out_shape=out_shapes,
        compiler_params=pltpu.CompilerParams(dimension_semantics=("arbitrary",),
                                             vmem_limit_bytes=VMEM_LIMIT),
        name="proj_mix_sample",
    )(*args)


def _pattn_kernel(pt_ref, q_ref, knew_ref, vnew_ref, cq_ref, cknew_ref, *rest, group, n_new):
    k_refs = rest[:group]
    v_refs = rest[group:2 * group]
    lf_refs = rest[2 * group:3 * group]
    o_ref, qbd_ref, m_ref, l_ref, acc_ref, carry_ref = rest[3 * group:]
    g = pl.program_id(1)
    rows = n_new * C_HEADS
    row_head = lax.broadcasted_iota(jnp.int32, (rows, C_WIDTH), 0) % C_HEADS
    lane_head = lax.broadcasted_iota(jnp.int32, (rows, C_WIDTH), 1) // HEAD_DIM
    own = row_head == lane_head

    @pl.when(g == 0)
    def _():
        qs = q_ref[0] * (HEAD_DIM ** -0.5)
        qrep = jnp.concatenate([jnp.broadcast_to(qs[t:t + 1], (C_HEADS, C_WIDTH)) for t in range(n_new)],
                               axis=0)
        qbd_ref[...] = jnp.where(own, qrep, 0.0).astype(BF16)
        m_ref[...] = jnp.full_like(m_ref, NEG_BIG)
        l_ref[...] = jnp.zeros_like(l_ref)
        acc_ref[...] = jnp.zeros_like(acc_ref)
        carry_ref[...] = jnp.zeros_like(carry_ref)

    qbd = qbd_ref[...]
    cq = cq_ref[0]

    def update(scores, pv):
        m_run = m_ref[...]
        m_cur = m_run
        for s in scores:
            m_cur = jnp.maximum(m_cur, jnp.max(s, axis=1, keepdims=True))
        alpha = jnp.exp(m_run - m_cur)
        l_new = alpha * l_ref[...]
        acc = alpha[:, 0:1] * acc_ref[...]
        for i, s in enumerate(scores):
            pr = jnp.exp(s - m_cur[:, 0:1])
            l_new = l_new + jnp.sum(pr, axis=1, keepdims=True)
            acc = acc + pv(i, pr.astype(BF16))
        m_ref[...] = m_cur
        l_ref[...] = l_new
        acc_ref[...] = acc

    page = lf_refs[0].shape[-1]
    sr = lax.broadcasted_iota(jnp.int32, (page, 2 * page), 0)
    sc = lax.broadcasted_iota(jnp.int32, (page, 2 * page), 1)
    later_or_all = jnp.where((sr > sc) | (sc >= page), 1.0, 0.0).astype(BF16)
    sums = _exact_rows_dot(jnp.concatenate([lf_refs[i][0, 0] for i in range(group)], axis=0), later_or_all)
    carry = carry_ref[...]
    scores = []
    for i in range(group):
        hs = slice(i * C_HEADS, (i + 1) * C_HEADS)
        bias = sums[hs, 0:page] + carry
        carry = carry + sums[hs, page:2 * page]
        s = _dot(qbd, k_refs[i][0, 0].astype(BF16))
        scores.append(s + jnp.concatenate([bias] * n_new, axis=0) + cq)
    carry_ref[...] = carry
    update(scores, lambda i, pr: _dot_nt(pr, v_refs[i][0, 0].astype(BF16)))

    @pl.when(g == pl.num_programs(1) - 1)
    def _():
        s = _dot_nt(qbd, knew_ref[0].astype(BF16))
        s = s + cq[:, 0:NEW_PAD] - cknew_ref[0]
        tq = lax.broadcasted_iota(jnp.int32, (rows, NEW_PAD), 0) // C_HEADS
        sk = lax.broadcasted_iota(jnp.int32, (rows, NEW_PAD), 1)
        s = jnp.where(sk <= tq, s, NEG_BIG)
        update([s], lambda i, pr: _dot(pr, vnew_ref[0].astype(BF16)))
        full = jnp.where(own, acc_ref[...] / l_ref[...][:, 0:1], 0.0)
        o_ref[0] = jnp.sum(full.reshape(n_new, C_HEADS, C_WIDTH), axis=1).astype(BF16)


def _paged_attn(page_table, q_b, knew_b, vnew_b, cq_rep, cknew, cache_kt, cache_vt, cache_lft, layer, group):
    db, n_pages = page_table.shape
    n_new = q_b.shape[1]
    page = cache_kt.shape[-1]
    rows = n_new * C_HEADS
    per_seq = lambda shape: pl.BlockSpec((1,) + shape, lambda b, g, pt: (b,) + (0,) * len(shape))

    def page_spec(i, height):
        def index(b, g, pt):
            return (layer, pt[b, n_pages - 1 - (g * group + i)], 0, 0)
        return pl.BlockSpec((1, 1, height, page), index)

    in_specs = [per_seq((n_new, C_WIDTH)), per_seq(knew_b.shape[1:]), per_seq(vnew_b.shape[1:]),
                per_seq((rows, LANES)), per_seq(cknew.shape[1:])]
    in_specs += [page_spec(i, C_WIDTH) for i in range(group)] * 2
    in_specs += [page_spec(i, C_HEADS) for i in range(group)]
    grid_spec = pltpu.PrefetchScalarGridSpec(
        num_scalar_prefetch=1,
        grid=(db, n_pages // group),
        in_specs=in_specs,
        out_specs=pl.BlockSpec((1, n_new, C_WIDTH), lambda b, g, pt: (b, 0, 0)),
        scratch_shapes=[pltpu.VMEM((rows, C_WIDTH), BF16), pltpu.VMEM((rows, LANES), F32),
                        pltpu.VMEM((rows, LANES), F32), pltpu.VMEM((rows, C_WIDTH), F32),
                        pltpu.VMEM((C_HEADS, page), F32)],
    )
    return pl.pallas_call(
        functools.partial(_pattn_kernel, group=group, n_new=n_new),
        grid_spec=grid_spec,
        out_shape=jax.ShapeDtypeStruct((db, n_new, C_WIDTH), BF16),
        compiler_params=pltpu.CompilerParams(dimension_semantics=("arbitrary", "arbitrary"),
                                             vmem_limit_bytes=VMEM_LIMIT),
        name="paged_forget_attn",
    )(page_table, q_b, knew_b, vnew_b, cq_rep, cknew,
      *([cache_kt] * group), *([cache_vt] * group), *([cache_lft] * group))


def _row_tile(n, target):
    t = min(n, target)
    while n % t:
        t //= 2
    return t


def kernel(x_prompt, x_sample, cache_k, cache_v, cache_logf, state_pool, page_table,
           ffn1_g_pre, ffn1_g_post, ffn1_w_gate, ffn1_w_up, ffn1_w_down,
           mix_g_pre, mix_g_post, w_in, w_out,
           chunk_v_g, chunk_w_s, chunk_b, pool_w, pool_scale, forget_b,
           ffn2_g_pre, ffn2_g_post, ffn2_w_gate, ffn2_w_up, ffn2_w_down):
    bsz, seq, d = x_prompt.shape
    db, n_new, _ = x_sample.shape
    depth, n_phys, page = cache_k.shape[:3]
    n_pages = page_table.shape[1]
    past = n_pages * page
    n = bsz * seq
    rows_s = db * n_new
    d_main = w_in.shape[-1] - C_HEADS

    tm_ffn = _row_tile(n, 512)
    tm_proj = _row_tile(seq, 512)
    blk = _row_tile(seq, 1024)
    group = _row_tile(n_pages, 16)

    xp = x_prompt.reshape(n, d)
    xs = jnp.transpose(x_sample, (1, 0, 2)).reshape(rows_s, d)
    cache_kt = jnp.transpose(cache_k, (0, 1, 3, 4, 2)).reshape(depth, n_phys, C_WIDTH, page)
    cache_vt = jnp.transpose(cache_v, (0, 1, 3, 4, 2)).reshape(depth, n_phys, C_WIDTH, page)
    cache_lft = jnp.transpose(cache_logf, (0, 1, 3, 2))

    outs = {k: [] for k in ("lfp", "poolp", "ks", "vs", "lfs", "pools", "cvs")}
    k_all = jnp.zeros((depth, bsz, HEAD_PAIRS, LANES, seq), F32)
    v_all = jnp.zeros((depth, bsz, HEAD_PAIRS, LANES, seq), F32)
    for l in range(depth):
        row2 = lambda v: v[l].reshape(1, -1)
        f1 = (row2(ffn1_g_pre), row2(ffn1_g_post), ffn1_w_gate[l].astype(BF16),
              ffn1_w_up[l].astype(BF16), ffn1_w_down[l].astype(BF16))
        f2 = (row2(ffn2_g_pre), row2(ffn2_g_post), ffn2_w_gate[l].astype(BF16),
              ffn2_w_up[l].astype(BF16), ffn2_w_down[l].astype(BF16))
        win = w_in[l, :, :d_main].astype(BF16)
        wfl = jnp.pad(w_in[l, :, d_main:], ((0, 0), (0, LANES - C_HEADS))).astype(BF16)
        fb = jnp.pad(forget_b[l], (0, LANES - C_HEADS)).reshape(1, LANES)
        wout = w_out[l].astype(BF16)
        bmat = jnp.repeat(chunk_b[l].T, HEAD_DIM, axis=1)
        pbd = jax.scipy.linalg.block_diag(*[pool_w[l, g] for g in range(len(POOL_WINDOWS))]).astype(BF16)
        mixp = (row2(mix_g_pre), win, wfl, fb, row2(chunk_v_g))
        pool_p = (pbd, row2(pool_scale))

        xp = _ffn(xp, *f1, tm_ffn)
        (a, b, qt, k2, kc, vt, k_all, v_all, lf_t, crow, tail) = _proj_prompt(
            xp, bsz, seq, *mixp, chunk_w_s[l], bmat, *pool_p, k_all, v_all, l, tm_proj)
        c = _attn_prompt(qt, k2, kc, vt, crow, bsz, seq, blk)
        xp = _mix_ffn(xp, a, b, c, wout, row2(mix_g_post), *f2, tm_ffn)
        outs["lfp"].append(jnp.transpose(lf_t, (0, 2, 1)))
        outs["poolp"].append(tail[:, POOL_CARRY - POOL_BUF:])

        xs = _ffn(xs, *f1, rows_s)
        wexp = jnp.repeat(jnp.repeat(chunk_w_s[l, :, :n_new, :n_new], db, axis=1), db, axis=2)
        bsamp = jnp.repeat(bmat[:n_new], db, axis=0)
        state_t = jnp.transpose(state_pool[l], (1, 0, 2))
        (a_s, b_s, q_s, k_s, v_s, lf_s, cnew, vn_s, pin_s) = _proj_sample(
            xs, *mixp, wexp, bsamp, *pool_p, state_t, db, n_new, past)
        to_b = lambda v: jnp.transpose(v.reshape(n_new, db, -1), (1, 0, 2))
        q_b, k_b, v_b, cnew_b = to_b(q_s), to_b(k_s), to_b(v_s), to_b(cnew)
        pad_new = ((0, 0), (0, NEW_PAD - n_new), (0, 0))
        cq_rep = jnp.broadcast_to(cnew_b.reshape(db, n_new * C_HEADS, 1), (db, n_new * C_HEADS, LANES))
        cknew = jnp.pad(jnp.tile(jnp.transpose(cnew_b, (0, 2, 1)), (1, n_new, 1)),
                        ((0, 0), (0, 0), (0, NEW_PAD - n_new)))
        c_s = _paged_attn(page_table, q_b, jnp.pad(k_b, pad_new), jnp.pad(v_b, pad_new), cq_rep, cknew,
                          cache_kt, cache_vt, cache_lft, l, group)
        c_s = jnp.transpose(c_s, (1, 0, 2)).reshape(rows_s, C_WIDTH)
        xs = _mix_ffn(xs, a_s, b_s, c_s, wout, row2(mix_g_post), *f2, rows_s)
        outs["ks"].append(k_b.reshape(db, n_new, C_HEADS, HEAD_DIM))
        outs["vs"].append(v_b.reshape(db, n_new, C_HEADS, HEAD_DIM))
        outs["lfs"].append(to_b(lf_s))
        outs["pools"].append(jnp.concatenate([state_pool[l], to_b(pin_s)], axis=1)[:, -POOL_BUF:])
        outs["cvs"].append(to_b(vn_s))

    y_prompt = xp.reshape(bsz, seq, d)
    y_sample = jnp.transpose(xs.reshape(n_new, db, d), (1, 0, 2))
    st = lambda key: jnp.stack(outs[key])
    to_heads = lambda t: jnp.transpose(t.reshape(depth, bsz, C_HEADS, HEAD_DIM, seq), (0, 1, 4, 2, 3))
    return (y_prompt, y_sample, to_heads(k_all), to_heads(v_all), st("lfp"), st("poolp"),
            st("ks"), st("vs"), st("lfs"), st("pools"), st("cvs"))
```

```python
import functools

import jax
import jax.numpy as jnp
from jax import lax
from jax.experimental import pallas as pl
from jax.experimental.pallas import tpu as pltpu

F32 = jnp.float32
BF16 = jnp.bfloat16

RMS_EPS = 1e-6
FFN_RES = 0.5
HEAD_DIM = 64
CHUNK = 128
POOL_WINDOWS = (2, 4, 8, 16)
POOL_BUF = max(POOL_WINDOWS) - 1
POOL_CARRY = POOL_BUF + 1
A_WIDTH = 256
A_HEADS = A_WIDTH // HEAD_DIM
B_WIDTH = 256
C_HEADS = 8
C_WIDTH = C_HEADS * HEAD_DIM
LANES = 128
HEAD_PAIRS = C_WIDTH // LANES
NEW_PAD = 16
DENOM_ROWS = 16
LOG2E = 1.4426950408889634
DIAG_TILE = 256
STALE_MAX_SLACK = 16.0
NEG_BIG = -1e30
VMEM_LIMIT = 56 * 1024 * 1024


def _resident(shape):
    nd = len(shape)
    return pl.BlockSpec(shape, lambda *_: (0,) * nd, pipeline_mode=pl.Buffered(1))


def _layer_resident(shape, layer):
    nd = len(shape)
    return pl.BlockSpec((None,) + tuple(shape), lambda *_: (layer,) + (0,) * nd, pipeline_mode=pl.Buffered(1))


def _rms(x, g):
    return x * lax.rsqrt(jnp.mean(x * x, axis=-1, keepdims=True) + RMS_EPS) * g


def _split3(x):
    hi = x.astype(BF16)
    r = x - hi.astype(F32)
    mid = r.astype(BF16)
    lo = (r - mid.astype(F32)).astype(BF16)
    return hi, mid, lo


def _dot(a, b):
    return jnp.dot(a, b, preferred_element_type=F32)


def _dot_nt(a, b):
    return lax.dot_general(a, b, (((1,), (1,)), ((), ())), preferred_element_type=F32)


def _exact_rows_dot(x, ones_mat):
    rows = x.shape[0]
    hi, mid, lo = _split3(x)
    parts = jnp.concatenate([hi.astype(F32), mid.astype(F32), lo.astype(F32)], axis=0).astype(BF16)
    out = _dot(parts, ones_mat)
    return out[0:rows] + out[rows:2 * rows] + out[2 * rows:3 * rows]


def _half_ffn(x, gpre_ref, gpost_ref, wg_ref, wu_ref, wd_ref, f_chunks):
    hb = _rms(x, gpre_ref[...]).astype(BF16)
    f = None
    for c0, c1 in f_chunks:
        g = _dot(hb, wg_ref[:, c0:c1])
        u = _dot(hb, wu_ref[:, c0:c1])
        act = (g / (1.0 + jnp.exp(-g)) * u).astype(BF16)
        part = _dot(act, wd_ref[c0:c1, :])
        f = part if f is None else f + part
    return x + FFN_RES * _rms(f, gpost_ref[...])


def _ffn_kernel(x_ref, gpre_ref, gpost_ref, wg_ref, wu_ref, wd_ref, o_ref, *, f_chunks):
    o_ref[...] = _half_ffn(x_ref[...], gpre_ref, gpost_ref, wg_ref, wu_ref, wd_ref, f_chunks)


def _mix_ffn_kernel(x_ref, a_ref, b_ref, c_ref, wout_ref, gmix_ref, gpre_ref, gpost_ref, wg_ref, wu_ref,
                    wd_ref, o_ref, *, f_chunks):
    y = (_dot(a_ref[...], wout_ref[0:A_WIDTH, :])
         + _dot(b_ref[...], wout_ref[A_WIDTH:A_WIDTH + B_WIDTH, :])
         + _dot(c_ref[...], wout_ref[A_WIDTH + B_WIDTH:, :]))
    x = x_ref[...] + _rms(y, gmix_ref[...])
    o_ref[...] = _half_ffn(x, gpre_ref, gpost_ref, wg_ref, wu_ref, wd_ref, f_chunks)


def _ffn_chunks(f):
    half = (f // 2 + 255) // 256 * 256
    return ((0, half), (half, f)) if half < f else ((0, f),)


def _ffn(x, gpre, gpost, wg, wu, wd, layer, tm):
    n, d = x.shape
    f = wg.shape[-1]
    row = pl.BlockSpec((tm, d), lambda i: (i, 0))
    return pl.pallas_call(
        functools.partial(_ffn_kernel, f_chunks=_ffn_chunks(f)),
        grid=(n // tm,),
        in_specs=[row, _resident((1, d)), _resident((1, d)), _layer_resident((d, f), layer),
                  _layer_resident((d, f), layer), _layer_resident((f, d), layer)],
        out_specs=row,
        out_shape=jax.ShapeDtypeStruct((n, d), F32),
        compiler_params=pltpu.CompilerParams(dimension_semantics=("parallel",),
                                             vmem_limit_bytes=VMEM_LIMIT),
        name="half_ffn",
    )(x, gpre, gpost, wg, wu, wd)


def _mix_ffn(x, a, b, c, wout, gmix, gpre, gpost, wg, wu, wd, layer, tm):
    n, d = x.shape
    f = wg.shape[-1]
    row = lambda w: pl.BlockSpec((tm, w), lambda i: (i, 0))
    return pl.pallas_call(
        functools.partial(_mix_ffn_kernel, f_chunks=_ffn_chunks(f)),
        grid=(n // tm,),
        in_specs=[row(d), row(A_WIDTH), row(B_WIDTH), row(C_WIDTH), _layer_resident(wout.shape[1:], layer),
                  _resident((1, d)), _resident((1, d)), _resident((1, d)), _layer_resident((d, f), layer),
                  _layer_resident((d, f), layer), _layer_resident((f, d), layer)],
        out_specs=row(d),
        out_shape=jax.ShapeDtypeStruct((n, d), F32),
        compiler_params=pltpu.CompilerParams(dimension_semantics=("parallel",),
                                             vmem_limit_bytes=VMEM_LIMIT),
        name="mix_out_ffn",
    )(x, a, b, c, wout, gmix, gpre, gpost, wg, wu, wd)


def _log_sigmoid(z):
    return jnp.minimum(z, 0.0) - jnp.log1p(jnp.exp(-jnp.abs(z)))


def _head_rms(va, cvg):
    r = lax.broadcasted_iota(jnp.int32, (A_WIDTH, A_WIDTH), 0) // HEAD_DIM
    c = lax.broadcasted_iota(jnp.int32, (A_WIDTH, A_WIDTH), 1) // HEAD_DIM
    same_head = jnp.where(r == c, 1.0, 0.0).astype(BF16)
    x2 = va * va
    x2h = x2.astype(BF16)
    x2l = (x2 - x2h.astype(F32)).astype(BF16)
    ms = (_dot(x2h, same_head) + _dot(x2l, same_head)) * (1.0 / HEAD_DIM)
    return va * lax.rsqrt(ms + RMS_EPS) * cvg


def _gate_rows(w_heads, vnb, bias):
    rows = vnb.shape[0]
    lane_head = lax.broadcasted_iota(jnp.int32, (rows, A_WIDTH), 1) // HEAD_DIM
    z = bias
    for h in range(A_HEADS):
        z = z + jnp.where(lane_head == h, _dot(w_heads[h], vnb), 0.0)
    return z


def _pool_delta(delayed, count, half):
    w_small, w_big = POOL_WINDOWS[2 * half], POOL_WINDOWS[2 * half + 1]
    cur = delayed(0)
    acc = cur
    for j in range(1, w_small):
        acc = acc + delayed(j)
    acc_small = acc
    for j in range(w_small, w_big):
        acc = acc + delayed(j)
    is_small = lax.broadcasted_iota(jnp.int32, cur.shape, 1) < HEAD_DIM
    mean = jnp.where(is_small, acc_small, acc) / jnp.where(is_small, count(w_small), count(w_big))
    return mean - cur


def _proj_kernel(x_ref, gpre_ref, win_ref, wfl_ref, fb_ref, cvg_ref, ws_ref, bmat_ref, pbd_ref,
                 pscale_ref, k_all_ref, v_all_ref,
                 a_ref, b_ref, qt_ref, k2_ref, kc_ref, vt_ref, kout_ref, vout_ref, logf_ref, crow_ref,
                 tail_ref, ext_ref, carry_ref, *, tm):
    del k_all_ref, v_all_ref
    i = pl.program_id(1)
    x = x_ref[...]
    hb = _rms(x, gpre_ref[...]).astype(BF16)
    p = _dot(hb, win_ref[...])
    fl = _dot(hb, wfl_ref[...])
    u = p[:, 0:256]
    va = p[:, 256:512]
    pin = p[:, 512:768]
    q = p[:, 768:1280]
    k = p[:, 1280:1792]
    v = p[:, 1792:2304]

    k2_ref[...] = k.astype(BF16)
    for hp in range(HEAD_PAIRS):
        sl = slice(hp * LANES, (hp + 1) * LANES)
        qt_ref[0, hp] = (q[:, sl] * (HEAD_DIM ** -0.5 * LOG2E)).T.astype(BF16)
        kout_ref[0, 0, hp] = k[:, sl].T
        v_t = v[:, sl].T
        vout_ref[0, 0, hp] = v_t
        vt_ref[0, hp] = v_t.astype(BF16)

    lf_t = _log_sigmoid(fl + fb_ref[...]).T[0:C_HEADS, :]
    logf_ref[0] = lf_t
    r = lax.broadcasted_iota(jnp.int32, (tm, tm), 0)
    c = lax.broadcasted_iota(jnp.int32, (tm, tm), 1)
    upto = jnp.where(r <= c, 1.0, 0.0).astype(BF16)

    @pl.when(i == 0)
    def _():
        carry_ref[...] = jnp.zeros_like(carry_ref)

    csum = _exact_rows_dot(lf_t, upto) + carry_ref[:, 0:1]
    carry_ref[...] = jnp.broadcast_to(csum[:, tm - 1:tm], carry_ref.shape)
    csum2 = csum * LOG2E
    crow_ref[0] = csum2
    hi, mid, lo = _split3(csum2)
    cols = jnp.concatenate([hi.astype(F32), mid.astype(F32), lo.astype(F32),
                            jnp.zeros((LANES - 3 * C_HEADS, tm), F32)], axis=0)
    kc_ref[...] = cols.T.astype(BF16)

    vn = _head_rms(va, cvg_ref[...])
    vnb = vn.astype(BF16)
    rr = lax.broadcasted_iota(jnp.int32, (CHUNK, CHUNK), 0)
    cc = lax.broadcasted_iota(jnp.int32, (CHUNK, CHUNK), 1)
    w_heads = [jnp.where(cc <= rr, ws_ref[h], 0.0).astype(BF16) for h in range(A_HEADS)]
    for ci in range(tm // CHUNK):
        rs = slice(ci * CHUNK, (ci + 1) * CHUNK)
        z = _gate_rows(w_heads, vnb[rs], bmat_ref[...])
        a_ref[rs, :] = (u[rs] * z).astype(BF16)

    @pl.when(i == 0)
    def _():
        ext_ref[0:POOL_CARRY, :] = jnp.zeros((POOL_CARRY, B_WIDTH), F32)

    ext_ref[POOL_CARRY:POOL_CARRY + tm, :] = pin
    pos = i * tm + lax.broadcasted_iota(jnp.int32, (tm, LANES), 0)
    d_halves = []
    for half in range(2):
        ls = slice(half * LANES, (half + 1) * LANES)
        d_halves.append(_pool_delta(
            lambda j: ext_ref[pl.ds(POOL_CARRY - j, tm), ls],
            lambda w: jnp.minimum(pos + 1, w).astype(F32), half))
    d = jnp.concatenate(d_halves, axis=1).astype(BF16)
    b_ref[...] = (_dot(d, pbd_ref[...]) * pscale_ref[...]).astype(BF16)
    tail = ext_ref[tm:tm + POOL_CARRY, :]
    tail_ref[0] = tail
    ext_ref[0:POOL_CARRY, :] = tail


def _proj_prompt(x, bsz, seq, gpre, win, wfl, fb, cvg, ws, bmat, pbd, pscale, k_all, v_all, layer, tm):
    n, d = x.shape
    nt = seq // tm
    dp = win.shape[1]
    row = lambda w: pl.BlockSpec((tm, w), lambda b, i: (b * nt + i, 0))
    pair_t = pl.BlockSpec((1, HEAD_PAIRS, LANES, tm), lambda b, i: (b, 0, 0, i))
    layer_pair_t = pl.BlockSpec((1, 1, HEAD_PAIRS, LANES, tm), lambda b, i: (layer, b, 0, 0, i))
    head_t = pl.BlockSpec((1, C_HEADS, tm), lambda b, i: (b, 0, i))
    pair_shape = (bsz, HEAD_PAIRS, LANES, seq)
    untouched = pl.BlockSpec(memory_space=pl.ANY)
    out_shapes = (
        jax.ShapeDtypeStruct((n, A_WIDTH), BF16),
        jax.ShapeDtypeStruct((n, B_WIDTH), BF16),
        jax.ShapeDtypeStruct(pair_shape, BF16),
        jax.ShapeDtypeStruct((n, C_WIDTH), BF16),
        jax.ShapeDtypeStruct((n, LANES), BF16),
        jax.ShapeDtypeStruct(pair_shape, BF16),
        jax.ShapeDtypeStruct(k_all.shape, F32),
        jax.ShapeDtypeStruct(v_all.shape, F32),
        jax.ShapeDtypeStruct((bsz, C_HEADS, seq), F32),
        jax.ShapeDtypeStruct((bsz, C_HEADS, seq), F32),
        jax.ShapeDtypeStruct((bsz, POOL_CARRY, B_WIDTH), F32),
    )
    out_specs = (
        row(A_WIDTH), row(B_WIDTH), pair_t, row(C_WIDTH), row(LANES), pair_t, layer_pair_t, layer_pair_t,
        head_t, head_t,
        pl.BlockSpec((1, POOL_CARRY, B_WIDTH), lambda b, i: (b, 0, 0)),
    )
    return pl.pallas_call(
        functools.partial(_proj_kernel, tm=tm),
        grid=(bsz, nt),
        in_specs=[row(d), _resident((1, d)), _resident((d, dp)), _resident((d, LANES)),
                  _resident((1, LANES)), _resident((1, A_WIDTH)), _resident((A_HEADS, CHUNK, CHUNK)),
                  _resident((CHUNK, A_WIDTH)), _resident((B_WIDTH, B_WIDTH)), _resident((1, B_WIDTH)),
                  untouched, untouched],
        out_specs=out_specs,
        out_shape=out_shapes,
        input_output_aliases={10: 6, 11: 7},
        scratch_shapes=[pltpu.VMEM((tm + POOL_CARRY, B_WIDTH), F32), pltpu.VMEM((C_HEADS, LANES), F32)],
        compiler_params=pltpu.CompilerParams(dimension_semantics=("arbitrary", "arbitrary"),
                                             vmem_limit_bytes=VMEM_LIMIT),
        name="proj_mix",
    )(x, gpre, win, wfl, fb, cvg, ws, bmat, pbd, pscale, k_all, v_all)


def _attn_kernel(qt_ref, k2_ref, kc_ref, vt_ref, cq_ref, o_ref, *, blk):
    hp = pl.program_id(1)
    qi = pl.program_id(2)
    q2 = qt_ref[0, 0]
    row = lax.broadcasted_iota(jnp.int32, (LANES, blk), 0)
    qa, cq = [], []
    for hh in range(2):
        own_q = jnp.where((row // HEAD_DIM) == hh, q2, jnp.zeros_like(q2))
        minus_ck = jnp.where((row < 3 * C_HEADS) & ((row % C_HEADS) == 2 * hp + hh), -1.0, 0.0)
        qa.append(jnp.concatenate([own_q, minus_ck.astype(BF16)], axis=0))
        cq.append(cq_ref[0, 0, hh:hh + 1, :])
    ones = jnp.ones((DENOM_ROWS, blk), BF16)

    def load(j):
        off = pl.multiple_of(j * blk, blk)
        kk = jnp.concatenate([k2_ref[pl.ds(off, blk), :], kc_ref[pl.ds(off, blk), :]], axis=1)
        va = [jnp.concatenate([vt_ref[0, 0, hh * HEAD_DIM:(hh + 1) * HEAD_DIM, pl.ds(off, blk)], ones], axis=0)
              for hh in range(2)]
        return kk, va

    def exact(j, carry, masked):
        kk, va = load(j)
        ts = [_dot(kk, qa[hh]) for hh in range(2)]
        new = []
        for hh in range(2):
            m_run, acc = carry[hh]
            t = ts[hh]
            if masked:
                kpos = lax.broadcasted_iota(jnp.int32, (blk, blk), 0)
                qpos = lax.broadcasted_iota(jnp.int32, (blk, blk), 1)
                t = jnp.where(kpos <= qpos, t, NEG_BIG)
            m_new = jnp.maximum(m_run, jnp.max(t, axis=0, keepdims=True) + cq[hh])
            pr = jnp.exp2(t - (m_new - cq[hh])).astype(BF16)
            new.append((m_new, jnp.exp2(m_run - m_new) * acc + _dot(va[hh], pr)))
        return tuple(new)

    init = tuple((jnp.full((1, blk), NEG_BIG, F32), jnp.zeros((HEAD_DIM + DENOM_ROWS, blk), F32))
                 for _ in range(2))

    def diagonal():
        kk, va = load(qi)
        strips = [(hh, s * DIAG_TILE, (s + 1) * DIAG_TILE) for hh in range(2) for s in range(blk // DIAG_TILE)]
        t_diag = [_dot(kk[lo:hi], qa[hh][:, lo:hi]) for hh, lo, hi in strips]
        t_old = [_dot(kk[0:lo], qa[hh][:, lo:hi]) if lo else None for hh, lo, hi in strips]
        kpos = lax.broadcasted_iota(jnp.int32, (DIAG_TILE, DIAG_TILE), 0)
        qpos = lax.broadcasted_iota(jnp.int32, (DIAG_TILE, DIAG_TILE), 1)
        m_tile, m_fin, p_diag, p_old, rise = [], [], [], [], []
        for (hh, lo, hi), td, to in zip(strips, t_diag, t_old):
            cq_s = cq[hh][:, lo:hi]
            td = jnp.where(kpos <= qpos, td, NEG_BIG)
            m_s = jnp.max(td, axis=0, keepdims=True) + cq_s
            p_diag.append(jnp.exp2(td - (m_s - cq_s)).astype(BF16))
            m_tile.append(m_s)
            if to is None:
                p_old.append(None)
                m_fin.append(m_s)
            else:
                p_old.append(jnp.exp2(to - (m_s - cq_s)).astype(BF16))
                m_fin.append(jnp.maximum(m_s, jnp.max(to, axis=0, keepdims=True) + cq_s))
                rise.append(m_fin[-1] - m_s)
        accs = []
        for (hh, lo, hi), pd, po, m_s, m_f in zip(strips, p_diag, p_old, m_tile, m_fin):
            acc_s = _dot(va[hh][:, lo:hi], pd)
            if po is not None:
                acc_s = jnp.exp2(m_s - m_f) * (acc_s + _dot(va[hh][:, 0:lo], po))
            accs.append(acc_s)
        per_head = blk // DIAG_TILE
        new = tuple((jnp.concatenate(m_fin[hh * per_head:(hh + 1) * per_head], axis=1),
                     jnp.concatenate(accs[hh * per_head:(hh + 1) * per_head], axis=1)) for hh in range(2))
        ok = jnp.all(jnp.concatenate(rise, axis=1) <= STALE_MAX_SLACK)
        return lax.cond(ok, lambda: new, lambda: exact(qi, init, True))

    carry = diagonal()

    def older(i, carry):
        j = qi - 1 - i
        kk, va = load(j)
        new, rise = [], []
        for hh in range(2):
            m_run, acc = carry[hh]
            t = _dot(kk, qa[hh])
            pr = jnp.exp2(t - (m_run - cq[hh])).astype(BF16)
            m_new = jnp.maximum(m_run, jnp.max(t, axis=0, keepdims=True) + cq[hh])
            new.append((m_new, jnp.exp2(m_run - m_new) * (acc + _dot(va[hh], pr))))
            rise.append(m_new - m_run)
        ok = jnp.all(jnp.concatenate(rise, axis=1) <= STALE_MAX_SLACK)
        return lax.cond(ok, lambda: tuple(new), lambda: exact(j, carry, False))

    carry = lax.fori_loop(0, qi, older, carry)
    out_t = jnp.concatenate([acc[0:HEAD_DIM] / acc[HEAD_DIM:HEAD_DIM + 1] for _, acc in carry], axis=0)
    o_ref[...] = out_t.T.astype(BF16)


def _attn_prompt(qt, k2, kc, vt, crow, bsz, seq, blk):
    n = bsz * seq
    nq = seq // blk
    crow4 = crow.reshape(bsz, HEAD_PAIRS, 2, seq)
    return pl.pallas_call(
        functools.partial(_attn_kernel, blk=blk),
        grid=(bsz, HEAD_PAIRS, nq),
        in_specs=[
            pl.BlockSpec((1, 1, LANES, blk), lambda b, hp, qi: (b, hp, 0, qi)),
            pl.BlockSpec((seq, LANES), lambda b, hp, qi: (b, hp)),
            pl.BlockSpec((seq, LANES), lambda b, hp, qi: (b, 0)),
            pl.BlockSpec((1, 1, LANES, seq), lambda b, hp, qi: (b, hp, 0, 0)),
            pl.BlockSpec((1, 1, 2, blk), lambda b, hp, qi: (b, hp, 0, qi)),
        ],
        out_specs=pl.BlockSpec((blk, LANES), lambda b, hp, qi: (b * nq + qi, hp)),
        out_shape=jax.ShapeDtypeStruct((n, C_WIDTH), BF16),
        compiler_params=pltpu.CompilerParams(
            dimension_semantics=("arbitrary", "arbitrary", "arbitrary"), vmem_limit_bytes=VMEM_LIMIT),
        name="forget_attn",
    )(qt, k2, kc, vt, crow4)


def _sproj_kernel(x_ref, gpre_ref, win_ref, wfl_ref, fb_ref, cvg_ref, wexp_ref, bsamp_ref, pbd_ref,
                  pscale_ref, state_ref,
                  a_ref, b_ref, q_ref, k_ref, v_ref, logf_ref, cnew_ref, vn_ref, pin_ref,
                  *, db, n_new, past):
    rows = db * n_new
    hb = _rms(x_ref[...], gpre_ref[...]).astype(BF16)
    p = _dot(hb, win_ref[...])
    fl = _dot(hb, wfl_ref[...])
    u = p[:, 0:256]
    va = p[:, 256:512]
    pin = p[:, 512:768]
    q_ref[...] = p[:, 768:1280]
    k_ref[...] = p[:, 1280:1792]
    v_ref[...] = p[:, 1792:2304]
    pin_ref[...] = pin

    lf = _log_sigmoid(fl + fb_ref[...])
    logf_ref[...] = lf[:, 0:C_HEADS]
    run = lf[0:db]
    cnew_ref[0:db, :] = run[:, 0:C_HEADS]
    for t in range(1, n_new):
        run = run + lf[t * db:(t + 1) * db]
        cnew_ref[t * db:(t + 1) * db, :] = run[:, 0:C_HEADS]

    vn = _head_rms(va, cvg_ref[...])
    vn_ref[...] = vn
    rr = lax.broadcasted_iota(jnp.int32, (rows, rows), 0)
    cc = lax.broadcasted_iota(jnp.int32, (rows, rows), 1)
    keep = ((rr % db) == (cc % db)) & ((cc // db) <= (rr // db))
    w_heads = [jnp.where(keep, wexp_ref[h], 0.0).astype(BF16) for h in range(A_HEADS)]
    z = _gate_rows(w_heads, vn.astype(BF16), bsamp_ref[...])
    a_ref[...] = (u * z).astype(BF16)

    def ext(idx, ls):
        if idx < POOL_BUF:
            return state_ref[idx][:, ls]
        t = idx - POOL_BUF
        return pin[t * db:(t + 1) * db, ls]

    d_rows = []
    for t in range(n_new):
        d_halves = []
        for half in range(2):
            ls = slice(half * LANES, (half + 1) * LANES)
            d_halves.append(_pool_delta(lambda j: ext(POOL_BUF + t - j, ls),
                                        lambda w: float(min(past + t + 1, w)), half))
        d_rows.append(jnp.concatenate(d_halves, axis=1))
    d = jnp.concatenate(d_rows, axis=0).astype(BF16)
    b_ref[...] = (_dot(d, pbd_ref[...]) * pscale_ref[...]).astype(BF16)


def _proj_sample(x, gpre, win, wfl, fb, cvg, wexp, bsamp, pbd, pscale, state_t, db, n_new, past):
    rows, d = x.shape
    out_shapes = (
        jax.ShapeDtypeStruct((rows, A_WIDTH), BF16),
        jax.ShapeDtypeStruct((rows, B_WIDTH), BF16),
        jax.ShapeDtypeStruct((rows, C_WIDTH), F32),
        jax.ShapeDtypeStruct((rows, C_WIDTH), F32),
        jax.ShapeDtypeStruct((rows, C_WIDTH), F32),
        jax.ShapeDtypeStruct((rows, C_HEADS), F32),
        jax.ShapeDtypeStruct((rows, C_HEADS), F32),
        jax.ShapeDtypeStruct((rows, A_WIDTH), F32),
        jax.ShapeDtypeStruct((rows, B_WIDTH), F32),
    )
    args = (x, gpre, win, wfl, fb, cvg, wexp, bsamp, pbd, pscale, state_t)
    return pl.pallas_call(
        functools.partial(_sproj_kernel, db=db, n_new=n_new, past=past),
        grid=(1,),
        in_specs=[pl.BlockSpec(a.shape, lambda i, nd=a.ndim: (0,) * nd) for a in args],
        out_specs=tuple(pl.BlockSpec(s.shape, lambda i: (0, 0)) for s in out_shapes),
        out_shape=out_shapes,
        compiler_params=pltpu.CompilerParams(dimension_semantics=("arbitrary",),
                                             vmem_limit_bytes=VMEM_LIMIT),
        name="proj_mix_sample",
    )(*args)


def _pattn_kernel(pt_ref, q_ref, knew_ref, vnew_ref, cq_ref, cknew_ref, kt_hbm, vt_hbm, lft_hbm,
                  o_ref, kbuf, vbuf, lfbuf, sems, *, layer, group, n_pages, n_new):
    b = pl.program_id(0)
    n_seq = pl.num_programs(0)
    n_steps = n_pages // group
    rows = n_new * C_HEADS
    page = lfbuf.shape[-1]

    def copies(seq, step, slot):
        out = []
        for i in range(group):
            phys = pt_ref[seq, n_pages - 1 - (step * group + i)]
            out.append(pltpu.make_async_copy(kt_hbm.at[layer, phys], kbuf.at[slot, i], sems.at[slot, 0]))
            out.append(pltpu.make_async_copy(vt_hbm.at[layer, phys], vbuf.at[slot, i], sems.at[slot, 1]))
            out.append(pltpu.make_async_copy(lft_hbm.at[layer, phys], lfbuf.at[slot, i], sems.at[slot, 2]))
        return out

    @pl.when(b == 0)
    def _():
        for cp in copies(0, 0, 0):
            cp.start()

    row_head = lax.broadcasted_iota(jnp.int32, (rows, C_WIDTH), 0) % C_HEADS
    lane_head = lax.broadcasted_iota(jnp.int32, (rows, C_WIDTH), 1) // HEAD_DIM
    own = row_head == lane_head
    qs = q_ref[0] * (HEAD_DIM ** -0.5)
    qrep = jnp.concatenate([jnp.broadcast_to(qs[t:t + 1], (C_HEADS, C_WIDTH)) for t in range(n_new)], axis=0)
    qbd = jnp.where(own, qrep, 0.0).astype(BF16)
    cq = cq_ref[0]
    sr = lax.broadcasted_iota(jnp.int32, (page, 2 * page), 0)
    sc = lax.broadcasted_iota(jnp.int32, (page, 2 * page), 1)
    later_or_all = jnp.where((sr > sc) | (sc >= page), 1.0, 0.0).astype(BF16)

    def update(state, scores, pv):
        m_run, l_run, acc = state
        m_cur = m_run
        for s in scores:
            m_cur = jnp.maximum(m_cur, jnp.max(s, axis=1, keepdims=True))
        alpha = jnp.exp(m_run - m_cur)
        l_new = alpha * l_run
        acc = alpha * acc
        for i, s in enumerate(scores):
            pr = jnp.exp(s - m_cur)
            l_new = l_new + jnp.sum(pr, axis=1, keepdims=True)
            acc = acc + pv(i, pr.astype(BF16))
        return m_cur, l_new, acc

    def step(s, carry):
        state, later = carry
        slot = (b * n_steps + s) % 2

        @pl.when(s + 1 < n_steps)
        def _():
            for cp in copies(b, s + 1, 1 - slot):
                cp.start()

        @pl.when((s + 1 == n_steps) & (b + 1 < n_seq))
        def _():
            for cp in copies(b + 1, 0, 1 - slot):
                cp.start()

        for cp in copies(b, s, slot):
            cp.wait()

        sums = _exact_rows_dot(jnp.concatenate([lfbuf[slot, i] for i in range(group)], axis=0), later_or_all)
        scores = []
        for i in range(group):
            hs = slice(i * C_HEADS, (i + 1) * C_HEADS)
            bias = sums[hs, 0:page] + later
            later = later + sums[hs, page:2 * page]
            sco = _dot(qbd, kbuf[slot, i].astype(BF16))
            scores.append(sco + jnp.concatenate([bias] * n_new, axis=0) + cq)
        state = update(state, scores, lambda i, pr: _dot_nt(pr, vbuf[slot, i].astype(BF16)))
        return state, later

    init = ((jnp.full((rows, 1), NEG_BIG, F32), jnp.zeros((rows, 1), F32), jnp.zeros((rows, C_WIDTH), F32)),
            jnp.zeros((C_HEADS, page), F32))
    state, _ = lax.fori_loop(0, n_steps, step, init)

    s_new = _dot_nt(qbd, knew_ref[0].astype(BF16))
    s_new = s_new + cq[:, 0:NEW_PAD] - cknew_ref[0]
    tq = lax.broadcasted_iota(jnp.int32, (rows, NEW_PAD), 0) // C_HEADS
    sk = lax.broadcasted_iota(jnp.int32, (rows, NEW_PAD), 1)
    s_new = jnp.where(sk <= tq, s_new, NEG_BIG)
    _, l_fin, acc = update(state, [s_new], lambda i, pr: _dot(pr, vnew_ref[0].astype(BF16)))
    full = jnp.where(own, acc / l_fin, 0.0)
    o_ref[0] = jnp.sum(full.reshape(n_new, C_HEADS, C_WIDTH), axis=1).astype(BF16)


def _paged_attn(page_table, q_b, knew_b, vnew_b, cq_rep, cknew, cache_kt, cache_vt, cache_lft, layer, group):
    db, n_pages = page_table.shape
    n_new = q_b.shape[1]
    page = cache_kt.shape[-1]
    rows = n_new * C_HEADS
    per_seq = lambda shape: pl.BlockSpec((1,) + shape, lambda b, pt: (b,) + (0,) * len(shape))
    in_hbm = pl.BlockSpec(memory_space=pl.ANY)
    grid_spec = pltpu.PrefetchScalarGridSpec(
        num_scalar_prefetch=1,
        grid=(db,),
        in_specs=[per_seq((n_new, C_WIDTH)), per_seq(knew_b.shape[1:]), per_seq(vnew_b.shape[1:]),
                  per_seq((rows, LANES)), per_seq(cknew.shape[1:]), in_hbm, in_hbm, in_hbm],
        out_specs=pl.BlockSpec((1, n_new, C_WIDTH), lambda b, pt: (b, 0, 0)),
        scratch_shapes=[pltpu.VMEM((2, group, C_WIDTH, page), F32), pltpu.VMEM((2, group, C_WIDTH, page), F32),
                        pltpu.VMEM((2, group, C_HEADS, page), F32), pltpu.SemaphoreType.DMA((2, 3))],
    )
    return pl.pallas_call(
        functools.partial(_pattn_kernel, layer=layer, group=group, n_pages=n_pages, n_new=n_new),
        grid_spec=grid_spec,
        out_shape=jax.ShapeDtypeStruct((db, n_new, C_WIDTH), BF16),
        compiler_params=pltpu.CompilerParams(dimension_semantics=("arbitrary",),
                                             vmem_limit_bytes=VMEM_LIMIT),
        name="paged_forget_attn",
    )(page_table, q_b, knew_b, vnew_b, cq_rep, cknew, cache_kt, cache_vt, cache_lft)


def _row_tile(n, target):
    t = min(n, target)
    while n % t:
        t //= 2
    return t


def kernel(x_prompt, x_sample, cache_k, cache_v, cache_logf, state_pool, page_table,
           ffn1_g_pre, ffn1_g_post, ffn1_w_gate, ffn1_w_up, ffn1_w_down,
           mix_g_pre, mix_g_post, w_in, w_out,
           chunk_v_g, chunk_w_s, chunk_b, pool_w, pool_scale, forget_b,
           ffn2_g_pre, ffn2_g_post, ffn2_w_gate, ffn2_w_up, ffn2_w_down):
    bsz, seq, d = x_prompt.shape
    db, n_new, _ = x_sample.shape
    depth, n_phys, page = cache_k.shape[:3]
    n_pages = page_table.shape[1]
    past = n_pages * page
    n = bsz * seq
    rows_s = db * n_new
    d_main = w_in.shape[-1] - C_HEADS

    tm_ffn = _row_tile(n, 512)
    tm_proj = _row_tile(seq, 512)
    blk = _row_tile(seq, 1024)
    group = _row_tile(n_pages, 16)

    xp = x_prompt.reshape(n, d)
    xs = jnp.transpose(x_sample, (1, 0, 2)).reshape(rows_s, d)
    cache_kt = jnp.transpose(cache_k, (0, 1, 3, 4, 2)).reshape(depth, n_phys, C_WIDTH, page)
    cache_vt = jnp.transpose(cache_v, (0, 1, 3, 4, 2)).reshape(depth, n_phys, C_WIDTH, page)
    cache_lft = jnp.transpose(cache_logf, (0, 1, 3, 2))

    outs = {k: [] for k in ("lfp", "poolp", "ks", "vs", "lfs", "pools", "cvs")}
    k_all = jnp.zeros((depth, bsz, HEAD_PAIRS, LANES, seq), F32)
    v_all = jnp.zeros((depth, bsz, HEAD_PAIRS, LANES, seq), F32)
    ffn1_w = tuple(w.astype(BF16) for w in (ffn1_w_gate, ffn1_w_up, ffn1_w_down))
    ffn2_w = tuple(w.astype(BF16) for w in (ffn2_w_gate, ffn2_w_up, ffn2_w_down))
    wout = w_out.astype(BF16)
    for l in range(depth):
        row2 = lambda v: v[l].reshape(1, -1)
        f1 = (row2(ffn1_g_pre), row2(ffn1_g_post)) + ffn1_w + (l,)
        f2 = (row2(ffn2_g_pre), row2(ffn2_g_post)) + ffn2_w + (l,)
        win = w_in[l, :, :d_main].astype(BF16)
        wfl = jnp.pad(w_in[l, :, d_main:], ((0, 0), (0, LANES - C_HEADS))).astype(BF16)
        fb = jnp.pad(forget_b[l], (0, LANES - C_HEADS)).reshape(1, LANES)
        bmat = jnp.repeat(chunk_b[l].T, HEAD_DIM, axis=1)
        pbd = jax.scipy.linalg.block_diag(*[pool_w[l, g] for g in range(len(POOL_WINDOWS))]).astype(BF16)
        mixp = (row2(mix_g_pre), win, wfl, fb, row2(chunk_v_g))
        pool_p = (pbd, row2(pool_scale))

        xp = _ffn(xp, *f1, tm_ffn)
        (a, b, qt, k2, kc, vt, k_all, v_all, lf_t, crow, tail) = _proj_prompt(
            xp, bsz, seq, *mixp, chunk_w_s[l], bmat, *pool_p, k_all, v_all, l, tm_proj)
        c = _attn_prompt(qt, k2, kc, vt, crow, bsz, seq, blk)
        xp = _mix_ffn(xp, a, b, c, wout, row2(mix_g_post), *f2, tm_ffn)
        outs["lfp"].append(jnp.transpose(lf_t, (0, 2, 1)))
        outs["poolp"].append(tail[:, POOL_CARRY - POOL_BUF:])

        xs = _ffn(xs, *f1, rows_s)
        wexp = jnp.repeat(jnp.repeat(chunk_w_s[l, :, :n_new, :n_new], db, axis=1), db, axis=2)
        bsamp = jnp.repeat(bmat[:n_new], db, axis=0)
        state_t = jnp.transpose(state_pool[l], (1, 0, 2))
        (a_s, b_s, q_s, k_s, v_s, lf_s, cnew, vn_s, pin_s) = _proj_sample(
            xs, *mixp, wexp, bsamp, *pool_p, state_t, db, n_new, past)
        to_b = lambda v: jnp.transpose(v.reshape(n_new, db, -1), (1, 0, 2))
        q_b, k_b, v_b, cnew_b = to_b(q_s), to_b(k_s), to_b(v_s), to_b(cnew)
        pad_new = ((0, 0), (0, NEW_PAD - n_new), (0, 0))
        cq_rep = jnp.broadcast_to(cnew_b.reshape(db, n_new * C_HEADS, 1), (db, n_new * C_HEADS, LANES))
        cknew = jnp.pad(jnp.tile(jnp.transpose(cnew_b, (0, 2, 1)), (1, n_new, 1)),
                        ((0, 0), (0, 0), (0, NEW_PAD - n_new)))
        c_s = _paged_attn(page_table, q_b, jnp.pad(k_b, pad_new), jnp.pad(v_b, pad_new), cq_rep, cknew,
                          cache_kt, cache_vt, cache_lft, l, group)
        c_s = jnp.transpose(c_s, (1, 0, 2)).reshape(rows_s, C_WIDTH)
        xs = _mix_ffn(xs, a_s, b_s, c_s, wout, row2(mix_g_post), *f2, rows_s)
        outs["ks"].append(k_b.reshape(db, n_new, C_HEADS, HEAD_DIM))
        outs["vs"].append(v_b.reshape(db, n_new, C_HEADS, HEAD_DIM))
        outs["lfs"].append(to_b(lf_s))
        outs["pools"].append(jnp.concatenate([state_pool[l], to_b(pin_s)], axis=1)[:, -POOL_BUF:])
        outs["cvs"].append(to_b(vn_s))

    y_prompt = xp.reshape(bsz, seq, d)
    y_sample = jnp.transpose(xs.reshape(n_new, db, d), (1, 0, 2))
    st = lambda key: jnp.stack(outs[key])
    to_heads = lambda t: jnp.transpose(t.reshape(depth, bsz, C_HEADS, HEAD_DIM, seq), (0, 1, 4, 2, 3))
    return (y_prompt, y_sample, to_heads(k_all), to_heads(v_all), st("lfp"), st("poolp"),
            st("ks"), st("vs"), st("lfs"), st("pools"), st("cvs"))
```

```python
import functools

import jax
import jax.numpy as jnp
from jax import lax
from jax.experimental import pallas as pl
from jax.experimental.pallas import tpu as pltpu

F32 = jnp.float32
BF16 = jnp.bfloat16

RMS_EPS = 1e-6
FFN_RES = 0.5
HEAD_DIM = 64
CHUNK = 128
POOL_WINDOWS = (2, 4, 8, 16)
POOL_BUF = max(POOL_WINDOWS) - 1
POOL_CARRY = POOL_BUF + 1
A_WIDTH = 256
A_HEADS = A_WIDTH // HEAD_DIM
B_WIDTH = 256
C_HEADS = 8
C_WIDTH = C_HEADS * HEAD_DIM
LANES = 128
HEAD_PAIRS = C_WIDTH // LANES
NEW_PAD = 16
DENOM_ROWS = 16
LOG2E = 1.4426950408889634
PROJ_SPLIT = 2
DIAG_TILE = 256
STALE_MAX_SLACK = 16.0
NEG_BIG = -1e30
VMEM_LIMIT = 56 * 1024 * 1024


def _resident(shape):
    nd = len(shape)
    return pl.BlockSpec(shape, lambda *_: (0,) * nd, pipeline_mode=pl.Buffered(1))


def _layer_resident(shape, layer):
    nd = len(shape)
    return pl.BlockSpec((None,) + tuple(shape), lambda *_: (layer,) + (0,) * nd, pipeline_mode=pl.Buffered(1))


def _rms(x, g):
    return x * lax.rsqrt(jnp.mean(x * x, axis=-1, keepdims=True) + RMS_EPS) * g


def _split3(x):
    hi = x.astype(BF16)
    r = x - hi.astype(F32)
    mid = r.astype(BF16)
    lo = (r - mid.astype(F32)).astype(BF16)
    return hi, mid, lo


def _dot(a, b):
    return jnp.dot(a, b, preferred_element_type=F32)


def _dot_nt(a, b):
    return lax.dot_general(a, b, (((1,), (1,)), ((), ())), preferred_element_type=F32)


def _exact_rows_dot(x, ones_mat):
    rows = x.shape[0]
    hi, mid, lo = _split3(x)
    parts = jnp.concatenate([hi.astype(F32), mid.astype(F32), lo.astype(F32)], axis=0).astype(BF16)
    out = _dot(parts, ones_mat)
    return out[0:rows] + out[rows:2 * rows] + out[2 * rows:3 * rows]


def _half_ffn(x, gpre_ref, gpost_ref, wg_ref, wu_ref, wd_ref, f_chunks):
    hb = _rms(x, gpre_ref[...]).astype(BF16)
    f = None
    for c0, c1 in f_chunks:
        g = _dot(hb, wg_ref[:, c0:c1])
        u = _dot(hb, wu_ref[:, c0:c1])
        act = (g / (1.0 + jnp.exp(-g)) * u).astype(BF16)
        part = _dot(act, wd_ref[c0:c1, :])
        f = part if f is None else f + part
    return x + FFN_RES * _rms(f, gpost_ref[...])


def _ffn_kernel(x_ref, gpre_ref, gpost_ref, wg_ref, wu_ref, wd_ref, o_ref, *, f_chunks):
    o_ref[...] = _half_ffn(x_ref[...], gpre_ref, gpost_ref, wg_ref, wu_ref, wd_ref, f_chunks)


def _mix_ffn_kernel(x_ref, a_ref, b_ref, c_ref, wout_ref, gmix_ref, gpre_ref, gpost_ref, wg_ref, wu_ref,
                    wd_ref, o_ref, *, f_chunks):
    y = (_dot(a_ref[...], wout_ref[0:A_WIDTH, :])
         + _dot(b_ref[...], wout_ref[A_WIDTH:A_WIDTH + B_WIDTH, :])
         + _dot(c_ref[...], wout_ref[A_WIDTH + B_WIDTH:, :]))
    x = x_ref[...] + _rms(y, gmix_ref[...])
    o_ref[...] = _half_ffn(x, gpre_ref, gpost_ref, wg_ref, wu_ref, wd_ref, f_chunks)


def _ffn_chunks(f):
    half = (f // 2 + 255) // 256 * 256
    return ((0, half), (half, f)) if half < f else ((0, f),)


def _ffn(x, gpre, gpost, wg, wu, wd, layer, tm):
    n, d = x.shape
    f = wg.shape[-1]
    row = pl.BlockSpec((tm, d), lambda i: (i, 0))
    return pl.pallas_call(
        functools.partial(_ffn_kernel, f_chunks=_ffn_chunks(f)),
        grid=(n // tm,),
        in_specs=[row, _resident((1, d)), _resident((1, d)), _layer_resident((d, f), layer),
                  _layer_resident((d, f), layer), _layer_resident((f, d), layer)],
        out_specs=row,
        out_shape=jax.ShapeDtypeStruct((n, d), F32),
        compiler_params=pltpu.CompilerParams(dimension_semantics=("parallel",),
                                             vmem_limit_bytes=VMEM_LIMIT),
        name="half_ffn",
    )(x, gpre, gpost, wg, wu, wd)


def _mix_ffn(x, a, b, c, wout, gmix, gpre, gpost, wg, wu, wd, layer, tm):
    n, d = x.shape
    f = wg.shape[-1]
    row = lambda w: pl.BlockSpec((tm, w), lambda i: (i, 0))
    return pl.pallas_call(
        functools.partial(_mix_ffn_kernel, f_chunks=_ffn_chunks(f)),
        grid=(n // tm,),
        in_specs=[row(d), row(A_WIDTH), row(B_WIDTH), row(C_WIDTH), _layer_resident(wout.shape[1:], layer),
                  _resident((1, d)), _resident((1, d)), _resident((1, d)), _layer_resident((d, f), layer),
                  _layer_resident((d, f), layer), _layer_resident((f, d), layer)],
        out_specs=row(d),
        out_shape=jax.ShapeDtypeStruct((n, d), F32),
        compiler_params=pltpu.CompilerParams(dimension_semantics=("parallel",),
                                             vmem_limit_bytes=VMEM_LIMIT),
        name="mix_out_ffn",
    )(x, a, b, c, wout, gmix, gpre, gpost, wg, wu, wd)


def _log_sigmoid(z):
    return jnp.minimum(z, 0.0) - jnp.log1p(jnp.exp(-jnp.abs(z)))


def _head_rms(va, cvg):
    r = lax.broadcasted_iota(jnp.int32, (A_WIDTH, A_WIDTH), 0) // HEAD_DIM
    c = lax.broadcasted_iota(jnp.int32, (A_WIDTH, A_WIDTH), 1) // HEAD_DIM
    same_head = jnp.where(r == c, 1.0, 0.0).astype(BF16)
    x2 = va * va
    x2h = x2.astype(BF16)
    x2l = (x2 - x2h.astype(F32)).astype(BF16)
    ms = (_dot(x2h, same_head) + _dot(x2l, same_head)) * (1.0 / HEAD_DIM)
    return va * lax.rsqrt(ms + RMS_EPS) * cvg


def _gate_rows(w_heads, vnb, bias):
    rows = vnb.shape[0]
    lane_head = lax.broadcasted_iota(jnp.int32, (rows, A_WIDTH), 1) // HEAD_DIM
    z = bias
    for h in range(A_HEADS):
        z = z + jnp.where(lane_head == h, _dot(w_heads[h], vnb), 0.0)
    return z


def _pool_delta(delayed, count, half):
    w_small, w_big = POOL_WINDOWS[2 * half], POOL_WINDOWS[2 * half + 1]
    cur = delayed(0)
    acc = cur
    for j in range(1, w_small):
        acc = acc + delayed(j)
    acc_small = acc
    for j in range(w_small, w_big):
        acc = acc + delayed(j)
    is_small = lax.broadcasted_iota(jnp.int32, cur.shape, 1) < HEAD_DIM
    mean = jnp.where(is_small, acc_small, acc) / jnp.where(is_small, count(w_small), count(w_big))
    return mean - cur


def _proj_kernel(x_ref, gpre_ref, win_ref, wfl_ref, fb_ref, cvg_ref, ws_ref, bmat_ref, pbd_ref,
                 pscale_ref, k_all_ref, v_all_ref,
                 a_ref, b_ref, qt_ref, k2_ref, kc_ref, vt_ref, kout_ref, vout_ref, logf_ref, crow_ref,
                 tail_ref, ext_ref, carry_ref, *, tm):
    del k_all_ref, v_all_ref
    i = pl.program_id(1)

    @pl.when(i == 0)
    def _():
        carry_ref[...] = jnp.zeros_like(carry_ref)
        ext_ref[0:POOL_CARRY, :] = jnp.zeros((POOL_CARRY, B_WIDTH), F32)

    hb = _rms(x_ref[...], gpre_ref[...]).astype(BF16)
    sub = tm // PROJ_SPLIT
    proj = [(_dot(hb[s * sub:(s + 1) * sub], win_ref[...]),
             _dot(hb[s * sub:(s + 1) * sub], wfl_ref[...]))
            for s in range(PROJ_SPLIT)]

    r = lax.broadcasted_iota(jnp.int32, (sub, sub), 0)
    c = lax.broadcasted_iota(jnp.int32, (sub, sub), 1)
    upto = jnp.where(r <= c, 1.0, 0.0).astype(BF16)
    rr = lax.broadcasted_iota(jnp.int32, (CHUNK, CHUNK), 0)
    cc = lax.broadcasted_iota(jnp.int32, (CHUNK, CHUNK), 1)
    w_heads = [jnp.where(cc <= rr, ws_ref[h], 0.0).astype(BF16) for h in range(A_HEADS)]

    for s, (p, fl) in enumerate(proj):
        r0 = s * sub
        rows = slice(r0, r0 + sub)
        u = p[:, 0:256]
        va = p[:, 256:512]
        pin = p[:, 512:768]
        q = p[:, 768:1280]
        k = p[:, 1280:1792]
        v = p[:, 1792:2304]

        k2_ref[rows, :] = k.astype(BF16)
        for hp in range(HEAD_PAIRS):
            sl = slice(hp * LANES, (hp + 1) * LANES)
            qt_ref[0, hp, :, rows] = (q[:, sl] * (HEAD_DIM ** -0.5 * LOG2E)).T.astype(BF16)
            kout_ref[0, 0, hp, :, rows] = k[:, sl].T
            v_t = v[:, sl].T
            vout_ref[0, 0, hp, :, rows] = v_t
            vt_ref[0, hp, :, rows] = v_t.astype(BF16)

        lf_t = _log_sigmoid(fl + fb_ref[...]).T[0:C_HEADS, :]
        logf_ref[0, :, rows] = lf_t
        csum = _exact_rows_dot(lf_t, upto) + carry_ref[:, 0:1]
        carry_ref[...] = jnp.broadcast_to(csum[:, sub - 1:sub], carry_ref.shape)
        csum2 = csum * LOG2E
        crow_ref[0, :, rows] = csum2
        hi, mid, lo = _split3(csum2)
        cols = jnp.concatenate([hi.astype(F32), mid.astype(F32), lo.astype(F32),
                                jnp.zeros((LANES - 3 * C_HEADS, sub), F32)], axis=0)
        kc_ref[rows, :] = cols.T.astype(BF16)

        vn = _head_rms(va, cvg_ref[...])
        vnb = vn.astype(BF16)
        for ci in range(sub // CHUNK):
            rs = slice(ci * CHUNK, (ci + 1) * CHUNK)
            z = _gate_rows(w_heads, vnb[rs], bmat_ref[...])
            a_ref[r0 + ci * CHUNK:r0 + (ci + 1) * CHUNK, :] = (u[rs] * z).astype(BF16)

        ext_ref[POOL_CARRY + r0:POOL_CARRY + r0 + sub, :] = pin
        pos = i * tm + r0 + lax.broadcasted_iota(jnp.int32, (sub, LANES), 0)
        d_halves = []
        for half in range(2):
            ls = slice(half * LANES, (half + 1) * LANES)
            d_halves.append(_pool_delta(
                lambda j: ext_ref[pl.ds(POOL_CARRY + r0 - j, sub), ls],
                lambda w: jnp.minimum(pos + 1, w).astype(F32), half))
        d = jnp.concatenate(d_halves, axis=1).astype(BF16)
        b_ref[rows, :] = (_dot(d, pbd_ref[...]) * pscale_ref[...]).astype(BF16)

    tail = ext_ref[tm:tm + POOL_CARRY, :]
    tail_ref[0] = tail
    ext_ref[0:POOL_CARRY, :] = tail


def _proj_prompt(x, bsz, seq, gpre, win, wfl, fb, cvg, ws, bmat, pbd, pscale, k_all, v_all, layer, tm):
    n, d = x.shape
    nt = seq // tm
    dp = win.shape[1]
    row = lambda w: pl.BlockSpec((tm, w), lambda b, i: (b * nt + i, 0))
    pair_t = pl.BlockSpec((1, HEAD_PAIRS, LANES, tm), lambda b, i: (b, 0, 0, i))
    layer_pair_t = pl.BlockSpec((1, 1, HEAD_PAIRS, LANES, tm), lambda b, i: (layer, b, 0, 0, i))
    head_t = pl.BlockSpec((1, C_HEADS, tm), lambda b, i: (b, 0, i))
    pair_shape = (bsz, HEAD_PAIRS, LANES, seq)
    untouched = pl.BlockSpec(memory_space=pl.ANY)
    out_shapes = (
        jax.ShapeDtypeStruct((n, A_WIDTH), BF16),
        jax.ShapeDtypeStruct((n, B_WIDTH), BF16),
        jax.ShapeDtypeStruct(pair_shape, BF16),
        jax.ShapeDtypeStruct((n, C_WIDTH), BF16),
        jax.ShapeDtypeStruct((n, LANES), BF16),
        jax.ShapeDtypeStruct(pair_shape, BF16),
        jax.ShapeDtypeStruct(k_all.shape, F32),
        jax.ShapeDtypeStruct(v_all.shape, F32),
        jax.ShapeDtypeStruct((bsz, C_HEADS, seq), F32),
        jax.ShapeDtypeStruct((bsz, C_HEADS, seq), F32),
        jax.ShapeDtypeStruct((bsz, POOL_CARRY, B_WIDTH), F32),
    )
    out_specs = (
        row(A_WIDTH), row(B_WIDTH), pair_t, row(C_WIDTH), row(LANES), pair_t, layer_pair_t, layer_pair_t,
        head_t, head_t,
        pl.BlockSpec((1, POOL_CARRY, B_WIDTH), lambda b, i: (b, 0, 0)),
    )
    return pl.pallas_call(
        functools.partial(_proj_kernel, tm=tm),
        grid=(bsz, nt),
        in_specs=[row(d), _resident((1, d)), _resident((d, dp)), _resident((d, LANES)),
                  _resident((1, LANES)), _resident((1, A_WIDTH)), _resident((A_HEADS, CHUNK, CHUNK)),
                  _resident((CHUNK, A_WIDTH)), _resident((B_WIDTH, B_WIDTH)), _resident((1, B_WIDTH)),
                  untouched, untouched],
        out_specs=out_specs,
        out_shape=out_shapes,
        input_output_aliases={10: 6, 11: 7},
        scratch_shapes=[pltpu.VMEM((tm + POOL_CARRY, B_WIDTH), F32), pltpu.VMEM((C_HEADS, LANES), F32)],
        compiler_params=pltpu.CompilerParams(dimension_semantics=("arbitrary", "arbitrary"),
                                             vmem_limit_bytes=VMEM_LIMIT),
        name="proj_mix",
    )(x, gpre, win, wfl, fb, cvg, ws, bmat, pbd, pscale, k_all, v_all)


def _attn_kernel(qt_ref, k2_ref, kc_ref, vt_ref, cq_ref, o_ref, *, blk):
    hp = pl.program_id(1)
    qi = pl.program_id(2)
    q2 = qt_ref[0, 0]
    row = lax.broadcasted_iota(jnp.int32, (LANES, blk), 0)
    qa, cq = [], []
    for hh in range(2):
        own_q = jnp.where((row // HEAD_DIM) == hh, q2, jnp.zeros_like(q2))
        minus_ck = jnp.where((row < 3 * C_HEADS) & ((row % C_HEADS) == 2 * hp + hh), -1.0, 0.0)
        qa.append(jnp.concatenate([own_q, minus_ck.astype(BF16)], axis=0))
        cq.append(cq_ref[0, 0, hh:hh + 1, :])
    ones = jnp.ones((DENOM_ROWS, blk), BF16)

    def load(j):
        off = pl.multiple_of(j * blk, blk)
        kk = jnp.concatenate([k2_ref[pl.ds(off, blk), :], kc_ref[pl.ds(off, blk), :]], axis=1)
        va = [jnp.concatenate([vt_ref[0, 0, hh * HEAD_DIM:(hh + 1) * HEAD_DIM, pl.ds(off, blk)], ones], axis=0)
              for hh in range(2)]
        return kk, va

    def exact(j, carry, masked):
        kk, va = load(j)
        ts = [_dot(kk, qa[hh]) for hh in range(2)]
        new = []
        for hh in range(2):
            m_run, acc = carry[hh]
            t = ts[hh]
            if masked:
                kpos = lax.broadcasted_iota(jnp.int32, (blk, blk), 0)
                qpos = lax.broadcasted_iota(jnp.int32, (blk, blk), 1)
                t = jnp.where(kpos <= qpos, t, NEG_BIG)
            m_new = jnp.maximum(m_run, jnp.max(t, axis=0, keepdims=True) + cq[hh])
            pr = jnp.exp2(t - (m_new - cq[hh])).astype(BF16)
            new.append((m_new, jnp.exp2(m_run - m_new) * acc + _dot(va[hh], pr)))
        return tuple(new)

    init = tuple((jnp.full((1, blk), NEG_BIG, F32), jnp.zeros((HEAD_DIM + DENOM_ROWS, blk), F32))
                 for _ in range(2))

    def diagonal():
        kk, va = load(qi)
        strips = [(hh, s * DIAG_TILE, (s + 1) * DIAG_TILE) for hh in range(2) for s in range(blk // DIAG_TILE)]
        t_diag = [_dot(kk[lo:hi], qa[hh][:, lo:hi]) for hh, lo, hi in strips]
        t_old = [_dot(kk[0:lo], qa[hh][:, lo:hi]) if lo else None for hh, lo, hi in strips]
        kpos = lax.broadcasted_iota(jnp.int32, (DIAG_TILE, DIAG_TILE), 0)
        qpos = lax.broadcasted_iota(jnp.int32, (DIAG_TILE, DIAG_TILE), 1)
        m_tile, m_fin, p_diag, p_old, rise = [], [], [], [], []
        for (hh, lo, hi), td, to in zip(strips, t_diag, t_old):
            cq_s = cq[hh][:, lo:hi]
            td = jnp.where(kpos <= qpos, td, NEG_BIG)
            m_s = jnp.max(td, axis=0, keepdims=True) + cq_s
            p_diag.append(jnp.exp2(td - (m_s - cq_s)).astype(BF16))
            m_tile.append(m_s)
            if to is None:
                p_old.append(None)
                m_fin.append(m_s)
            else:
                p_old.append(jnp.exp2(to - (m_s - cq_s)).astype(BF16))
                m_fin.append(jnp.maximum(m_s, jnp.max(to, axis=0, keepdims=True) + cq_s))
                rise.append(m_fin[-1] - m_s)
        accs = []
        for (hh, lo, hi), pd, po, m_s, m_f in zip(strips, p_diag, p_old, m_tile, m_fin):
            acc_s = _dot(va[hh][:, lo:hi], pd)
            if po is not None:
                acc_s = jnp.exp2(m_s - m_f) * (acc_s + _dot(va[hh][:, 0:lo], po))
            accs.append(acc_s)
        per_head = blk // DIAG_TILE
        new = tuple((jnp.concatenate(m_fin[hh * per_head:(hh + 1) * per_head], axis=1),
                     jnp.concatenate(accs[hh * per_head:(hh + 1) * per_head], axis=1)) for hh in range(2))
        ok = jnp.all(jnp.concatenate(rise, axis=1) <= STALE_MAX_SLACK)
        return lax.cond(ok, lambda: new, lambda: exact(qi, init, True))

    carry = diagonal()

    def older(i, carry):
        j = qi - 1 - i
        kk, va = load(j)
        new, rise = [], []
        for hh in range(2):
            m_run, acc = carry[hh]
            t = _dot(kk, qa[hh])
            pr = jnp.exp2(t - (m_run - cq[hh])).astype(BF16)
            m_new = jnp.maximum(m_run, jnp.max(t, axis=0, keepdims=True) + cq[hh])
            new.append((m_new, jnp.exp2(m_run - m_new) * (acc + _dot(va[hh], pr))))
            rise.append(m_new - m_run)
        ok = jnp.all(jnp.concatenate(rise, axis=1) <= STALE_MAX_SLACK)
        return lax.cond(ok, lambda: tuple(new), lambda: exact(j, carry, False))

    carry = lax.fori_loop(0, qi, older, carry)
    out_t = jnp.concatenate([acc[0:HEAD_DIM] / acc[HEAD_DIM:HEAD_DIM + 1] for _, acc in carry], axis=0)
    o_ref[...] = out_t.T.astype(BF16)


def _attn_prompt(qt, k2, kc, vt, crow, bsz, seq, blk):
    n = bsz * seq
    nq = seq // blk
    crow4 = crow.reshape(bsz, HEAD_PAIRS, 2, seq)
    return pl.pallas_call(
        functools.partial(_attn_kernel, blk=blk),
        grid=(bsz, HEAD_PAIRS, nq),
        in_specs=[
            pl.BlockSpec((1, 1, LANES, blk), lambda b, hp, qi: (b, hp, 0, qi)),
            pl.BlockSpec((seq, LANES), lambda b, hp, qi: (b, hp)),
            pl.BlockSpec((seq, LANES), lambda b, hp, qi: (b, 0)),
            pl.BlockSpec((1, 1, LANES, seq), lambda b, hp, qi: (b, hp, 0, 0)),
            pl.BlockSpec((1, 1, 2, blk), lambda b, hp, qi: (b, hp, 0, qi)),
        ],
        out_specs=pl.BlockSpec((blk, LANES), lambda b, hp, qi: (b * nq + qi, hp)),
        out_shape=jax.ShapeDtypeStruct((n, C_WIDTH), BF16),
        compiler_params=pltpu.CompilerParams(
            dimension_semantics=("arbitrary", "arbitrary", "arbitrary"), vmem_limit_bytes=VMEM_LIMIT),
        name="forget_attn",
    )(qt, k2, kc, vt, crow4)


def _sproj_kernel(x_ref, gpre_ref, win_ref, wfl_ref, fb_ref, cvg_ref, wexp_ref, bsamp_ref, pbd_ref,
                  pscale_ref, state_ref,
                  a_ref, b_ref, q_ref, k_ref, v_ref, logf_ref, cnew_ref, vn_ref, pin_ref,
                  *, db, n_new, past):
    rows = db * n_new
    hb = _rms(x_ref[...], gpre_ref[...]).astype(BF16)
    p = _dot(hb, win_ref[...])
    fl = _dot(hb, wfl_ref[...])
    u = p[:, 0:256]
    va = p[:, 256:512]
    pin = p[:, 512:768]
    q_ref[...] = p[:, 768:1280]
    k_ref[...] = p[:, 1280:1792]
    v_ref[...] = p[:, 1792:2304]
    pin_ref[...] = pin

    lf = _log_sigmoid(fl + fb_ref[...])
    logf_ref[...] = lf[:, 0:C_HEADS]
    run = lf[0:db]
    cnew_ref[0:db, :] = run[:, 0:C_HEADS]
    for t in range(1, n_new):
        run = run + lf[t * db:(t + 1) * db]
        cnew_ref[t * db:(t + 1) * db, :] = run[:, 0:C_HEADS]

    vn = _head_rms(va, cvg_ref[...])
    vn_ref[...] = vn
    rr = lax.broadcasted_iota(jnp.int32, (rows, rows), 0)
    cc = lax.broadcasted_iota(jnp.int32, (rows, rows), 1)
    keep = ((rr % db) == (cc % db)) & ((cc // db) <= (rr // db))
    w_heads = [jnp.where(keep, wexp_ref[h], 0.0).astype(BF16) for h in range(A_HEADS)]
    z = _gate_rows(w_heads, vn.astype(BF16), bsamp_ref[...])
    a_ref[...] = (u * z).astype(BF16)

    def ext(idx, ls):
        if idx < POOL_BUF:
            return state_ref[idx][:, ls]
        t = idx - POOL_BUF
        return pin[t * db:(t + 1) * db, ls]

    d_rows = []
    for t in range(n_new):
        d_halves = []
        for half in range(2):
            ls = slice(half * LANES, (half + 1) * LANES)
            d_halves.append(_pool_delta(lambda j: ext(POOL_BUF + t - j, ls),
                                        lambda w: float(min(past + t + 1, w)), half))
        d_rows.append(jnp.concatenate(d_halves, axis=1))
    d = jnp.concatenate(d_rows, axis=0).astype(BF16)
    b_ref[...] = (_dot(d, pbd_ref[...]) * pscale_ref[...]).astype(BF16)


def _proj_sample(x, gpre, win, wfl, fb, cvg, wexp, bsamp, pbd, pscale, state_t, db, n_new, past):
    rows, d = x.shape
    out_shapes = (
        jax.ShapeDtypeStruct((rows, A_WIDTH), BF16),
        jax.ShapeDtypeStruct((rows, B_WIDTH), BF16),
        jax.ShapeDtypeStruct((rows, C_WIDTH), F32),
        jax.ShapeDtypeStruct((rows, C_WIDTH), F32),
        jax.ShapeDtypeStruct((rows, C_WIDTH), F32),
        jax.ShapeDtypeStruct((rows, C_HEADS), F32),
        jax.ShapeDtypeStruct((rows, C_HEADS), F32),
        jax.ShapeDtypeStruct((rows, A_WIDTH), F32),
        jax.ShapeDtypeStruct((rows, B_WIDTH), F32),
    )
    args = (x, gpre, win, wfl, fb, cvg, wexp, bsamp, pbd, pscale, state_t)
    return pl.pallas_call(
        functools.partial(_sproj_kernel, db=db, n_new=n_new, past=past),
        grid=(1,),
        in_specs=[pl.BlockSpec(a.shape, lambda i, nd=a.ndim: (0,) * nd) for a in args],
        out_specs=tuple(pl.BlockSpec(s.shape, lambda i: (0, 0)) for s in out_shapes),
        out_shape=out_shapes,
        compiler_params=pltpu.CompilerParams(dimension_semantics=("arbitrary",),
                                             vmem_limit_bytes=VMEM_LIMIT),
        name="proj_mix_sample",
    )(*args)


def _pattn_kernel(pt_ref, q_ref, knew_ref, vnew_ref, cq_ref, cknew_ref, kt_hbm, vt_hbm, lft_hbm,
                  o_ref, kbuf, vbuf, lfbuf, sems, *, layer, group, n_pages, n_new):
    b = pl.program_id(0)
    n_seq = pl.num_programs(0)
    n_steps = n_pages // group
    rows = n_new * C_HEADS
    page = lfbuf.shape[-1]

    def copies(seq, step, slot):
        out = []
        for i in range(group):
            phys = pt_ref[seq, n_pages - 1 - (step * group + i)]
            out.append(pltpu.make_async_copy(kt_hbm.at[layer, phys], kbuf.at[slot, i], sems.at[slot, 0]))
            out.append(pltpu.make_async_copy(vt_hbm.at[layer, phys], vbuf.at[slot, i], sems.at[slot, 1]))
            out.append(pltpu.make_async_copy(lft_hbm.at[layer, phys], lfbuf.at[slot, i], sems.at[slot, 2]))
        return out

    @pl.when(b == 0)
    def _():
        for cp in copies(0, 0, 0):
            cp.start()

    row_head = lax.broadcasted_iota(jnp.int32, (rows, C_WIDTH), 0) % C_HEADS
    lane_head = lax.broadcasted_iota(jnp.int32, (rows, C_WIDTH), 1) // HEAD_DIM
    own = row_head == lane_head
    qs = q_ref[0] * (HEAD_DIM ** -0.5)
    qrep = jnp.concatenate([jnp.broadcast_to(qs[t:t + 1], (C_HEADS, C_WIDTH)) for t in range(n_new)], axis=0)
    qbd = jnp.where(own, qrep, 0.0).astype(BF16)
    cq = cq_ref[0]
    sr = lax.broadcasted_iota(jnp.int32, (page, 2 * page), 0)
    sc = lax.broadcasted_iota(jnp.int32, (page, 2 * page), 1)
    later_or_all = jnp.where((sr > sc) | (sc >= page), 1.0, 0.0).astype(BF16)

    def update(state, scores, pv):
        m_run, l_run, acc = state
        m_cur = m_run
        for s in scores:
            m_cur = jnp.maximum(m_cur, jnp.max(s, axis=1, keepdims=True))
        alpha = jnp.exp(m_run - m_cur)
        l_new = alpha * l_run
        acc = alpha * acc
        for i, s in enumerate(scores):
            pr = jnp.exp(s - m_cur)
            l_new = l_new + jnp.sum(pr, axis=1, keepdims=True)
            acc = acc + pv(i, pr.astype(BF16))
        return m_cur, l_new, acc

    def step(s, carry):
        state, later = carry
        slot = (b * n_steps + s) % 2

        @pl.when(s + 1 < n_steps)
        def _():
            for cp in copies(b, s + 1, 1 - slot):
                cp.start()

        @pl.when((s + 1 == n_steps) & (b + 1 < n_seq))
        def _():
            for cp in copies(b + 1, 0, 1 - slot):
                cp.start()

        for cp in copies(b, s, slot):
            cp.wait()

        sums = _exact_rows_dot(jnp.concatenate([lfbuf[slot, i] for i in range(group)], axis=0), later_or_all)
        scores = []
        for i in range(group):
            hs = slice(i * C_HEADS, (i + 1) * C_HEADS)
            bias = sums[hs, 0:page] + later
            later = later + sums[hs, page:2 * page]
            sco = _dot(qbd, kbuf[slot, i].astype(BF16))
            scores.append(sco + jnp.concatenate([bias] * n_new, axis=0) + cq)
        state = update(state, scores, lambda i, pr: _dot_nt(pr, vbuf[slot, i].astype(BF16)))
        return state, later

    init = ((jnp.full((rows, 1), NEG_BIG, F32), jnp.zeros((rows, 1), F32), jnp.zeros((rows, C_WIDTH), F32)),
            jnp.zeros((C_HEADS, page), F32))
    state, _ = lax.fori_loop(0, n_steps, step, init)

    s_new = _dot_nt(qbd, knew_ref[0].astype(BF16))
    s_new = s_new + cq[:, 0:NEW_PAD] - cknew_ref[0]
    tq = lax.broadcasted_iota(jnp.int32, (rows, NEW_PAD), 0) // C_HEADS
    sk = lax.broadcasted_iota(jnp.int32, (rows, NEW_PAD), 1)
    s_new = jnp.where(sk <= tq, s_new, NEG_BIG)
    _, l_fin, acc = update(state, [s_new], lambda i, pr: _dot(pr, vnew_ref[0].astype(BF16)))
    full = jnp.where(own, acc / l_fin, 0.0)
    o_ref[0] = jnp.sum(full.reshape(n_new, C_HEADS, C_WIDTH), axis=1).astype(BF16)


def _paged_attn(page_table, q_b, knew_b, vnew_b, cq_rep, cknew, cache_kt, cache_vt, cache_lft, layer, group):
    db, n_pages = page_table.shape
    n_new = q_b.shape[1]
    page = cache_kt.shape[-1]
    rows = n_new * C_HEADS
    per_seq = lambda shape: pl.BlockSpec((1,) + shape, lambda b, pt: (b,) + (0,) * len(shape))
    in_hbm = pl.BlockSpec(memory_space=pl.ANY)
    grid_spec = pltpu.PrefetchScalarGridSpec(
        num_scalar_prefetch=1,
        grid=(db,),
        in_specs=[per_seq((n_new, C_WIDTH)), per_seq(knew_b.shape[1:]), per_seq(vnew_b.shape[1:]),
                  per_seq((rows, LANES)), per_seq(cknew.shape[1:]), in_hbm, in_hbm, in_hbm],
        out_specs=pl.BlockSpec((1, n_new, C_WIDTH), lambda b, pt: (b, 0, 0)),
        scratch_shapes=[pltpu.VMEM((2, group, C_WIDTH, page), F32), pltpu.VMEM((2, group, C_WIDTH, page), F32),
                        pltpu.VMEM((2, group, C_HEADS, page), F32), pltpu.SemaphoreType.DMA((2, 3))],
    )
    return pl.pallas_call(
        functools.partial(_pattn_kernel, layer=layer, group=group, n_pages=n_pages, n_new=n_new),
        grid_spec=grid_spec,
        out_shape=jax.ShapeDtypeStruct((db, n_new, C_WIDTH), BF16),
        compiler_params=pltpu.CompilerParams(dimension_semantics=("arbitrary",),
                                             vmem_limit_bytes=VMEM_LIMIT),
        name="paged_forget_attn",
    )(page_table, q_b, knew_b, vnew_b, cq_rep, cknew, cache_kt, cache_vt, cache_lft)


def _row_tile(n, target):
    t = min(n, target)
    while n % t:
        t //= 2
    return t


def kernel(x_prompt, x_sample, cache_k, cache_v, cache_logf, state_pool, page_table,
           ffn1_g_pre, ffn1_g_post, ffn1_w_gate, ffn1_w_up, ffn1_w_down,
           mix_g_pre, mix_g_post, w_in, w_out,
           chunk_v_g, chunk_w_s, chunk_b, pool_w, pool_scale, forget_b,
           ffn2_g_pre, ffn2_g_post, ffn2_w_gate, ffn2_w_up, ffn2_w_down):
    bsz, seq, d = x_prompt.shape
    db, n_new, _ = x_sample.shape
    depth, n_phys, page = cache_k.shape[:3]
    n_pages = page_table.shape[1]
    past = n_pages * page
    n = bsz * seq
    rows_s = db * n_new
    d_main = w_in.shape[-1] - C_HEADS

    tm_ffn = _row_tile(n, 512)
    tm_proj = _row_tile(seq, 512)
    blk = _row_tile(seq, 1024)
    group = _row_tile(n_pages, 16)

    xp = x_prompt.reshape(n, d)
    xs = jnp.transpose(x_sample, (1, 0, 2)).reshape(rows_s, d)
    cache_kt = jnp.transpose(cache_k, (0, 1, 3, 4, 2)).reshape(depth, n_phys, C_WIDTH, page)
    cache_vt = jnp.transpose(cache_v, (0, 1, 3, 4, 2)).reshape(depth, n_phys, C_WIDTH, page)
    cache_lft = jnp.transpose(cache_logf, (0, 1, 3, 2))

    outs = {k: [] for k in ("lfp", "poolp", "ks", "vs", "lfs", "pools", "cvs")}
    k_all = jnp.zeros((depth, bsz, HEAD_PAIRS, LANES, seq), F32)
    v_all = jnp.zeros((depth, bsz, HEAD_PAIRS, LANES, seq), F32)
    ffn1_w = tuple(w.astype(BF16) for w in (ffn1_w_gate, ffn1_w_up, ffn1_w_down))
    ffn2_w = tuple(w.astype(BF16) for w in (ffn2_w_gate, ffn2_w_up, ffn2_w_down))
    wout = w_out.astype(BF16)
    for l in range(depth):
        row2 = lambda v: v[l].reshape(1, -1)
        f1 = (row2(ffn1_g_pre), row2(ffn1_g_post)) + ffn1_w + (l,)
        f2 = (row2(ffn2_g_pre), row2(ffn2_g_post)) + ffn2_w + (l,)
        win = w_in[l, :, :d_main].astype(BF16)
        wfl = jnp.pad(w_in[l, :, d_main:], ((0, 0), (0, LANES - C_HEADS))).astype(BF16)
        fb = jnp.pad(forget_b[l], (0, LANES - C_HEADS)).reshape(1, LANES)
        bmat = jnp.repeat(chunk_b[l].T, HEAD_DIM, axis=1)
        pbd = jax.scipy.linalg.block_diag(*[pool_w[l, g] for g in range(len(POOL_WINDOWS))]).astype(BF16)
        mixp = (row2(mix_g_pre), win, wfl, fb, row2(chunk_v_g))
        pool_p = (pbd, row2(pool_scale))

        xp = _ffn(xp, *f1, tm_ffn)
        (a, b, qt, k2, kc, vt, k_all, v_all, lf_t, crow, tail) = _proj_prompt(
            xp, bsz, seq, *mixp, chunk_w_s[l], bmat, *pool_p, k_all, v_all, l, tm_proj)
        c = _attn_prompt(qt, k2, kc, vt, crow, bsz, seq, blk)
        xp = _mix_ffn(xp, a, b, c, wout, row2(mix_g_post), *f2, tm_ffn)
        outs["lfp"].append(jnp.transpose(lf_t, (0, 2, 1)))
        outs["poolp"].append(tail[:, POOL_CARRY - POOL_BUF:])

        xs = _ffn(xs, *f1, rows_s)
        wexp = jnp.repeat(jnp.repeat(chunk_w_s[l, :, :n_new, :n_new], db, axis=1), db, axis=2)
        bsamp = jnp.repeat(bmat[:n_new], db, axis=0)
        state_t = jnp.transpose(state_pool[l], (1, 0, 2))
        (a_s, b_s, q_s, k_s, v_s, lf_s, cnew, vn_s, pin_s) = _proj_sample(
            xs, *mixp, wexp, bsamp, *pool_p, state_t, db, n_new, past)
        to_b = lambda v: jnp.transpose(v.reshape(n_new, db, -1), (1, 0, 2))
        q_b, k_b, v_b, cnew_b = to_b(q_s), to_b(k_s), to_b(v_s), to_b(cnew)
        pad_new = ((0, 0), (0, NEW_PAD - n_new), (0, 0))
        cq_rep = jnp.broadcast_to(cnew_b.reshape(db, n_new * C_HEADS, 1), (db, n_new * C_HEADS, LANES))
        cknew = jnp.pad(jnp.tile(jnp.transpose(cnew_b, (0, 2, 1)), (1, n_new, 1)),
                        ((0, 0), (0, 0), (0, NEW_PAD - n_new)))
        c_s = _paged_attn(page_table, q_b, jnp.pad(k_b, pad_new), jnp.pad(v_b, pad_new), cq_rep, cknew,
                          cache_kt, cache_vt, cache_lft, l, group)
        c_s = jnp.transpose(c_s, (1, 0, 2)).reshape(rows_s, C_WIDTH)
        xs = _mix_ffn(xs, a_s, b_s, c_s, wout, row2(mix_g_post), *f2, rows_s)
        outs["ks"].append(k_b.reshape(db, n_new, C_HEADS, HEAD_DIM))
        outs["vs"].append(v_b.reshape(db, n_new, C_HEADS, HEAD_DIM))
        outs["lfs"].append(to_b(lf_s))
        outs["pools"].append(jnp.concatenate([state_pool[l], to_b(pin_s)], axis=1)[:, -POOL_BUF:])
        outs["cvs"].append(to_b(vn_s))

    y_prompt = xp.reshape(bsz, seq, d)
    y_sample = jnp.transpose(xs.reshape(n_new, db, d), (1, 0, 2))
    st = lambda key: jnp.stack(outs[key])
    to_heads = lambda t: jnp.transpose(t.reshape(depth, bsz, C_HEADS, HEAD_DIM, seq), (0, 1, 4, 2, 3))
    return (y_prompt, y_sample, to_heads(k_all), to_heads(v_all), st("lfp"), st("poolp"),
            st("ks"), st("vs"), st("lfs"), st("pools"), st("cvs"))
```

```python
import functools

import jax
import jax.numpy as jnp
from jax import lax
from jax.experimental import pallas as pl
from jax.experimental.pallas import tpu as pltpu

F32 = jnp.float32
BF16 = jnp.bfloat16

RMS_EPS = 1e-6
FFN_RES = 0.5
HEAD_DIM = 64
CHUNK = 128
POOL_WINDOWS = (2, 4, 8, 16)
POOL_BUF = max(POOL_WINDOWS) - 1
POOL_CARRY = POOL_BUF + 1
A_WIDTH = 256
A_HEADS = A_WIDTH // HEAD_DIM
B_WIDTH = 256
C_HEADS = 8
C_WIDTH = C_HEADS * HEAD_DIM
LANES = 128
MXU_TILE = 256
BF16_SUBLANES = 16
HEAD_PAIRS = C_WIDTH // LANES
NEW_PAD = BF16_SUBLANES
DENOM_ROWS = BF16_SUBLANES
LOG2E = 1.4426950408889634
PROJ_SPLIT = 2
DIAG_TILE = MXU_TILE
STALE_MAX_SLACK = 16.0
NEG_BIG = -1e30
VMEM_LIMIT = 56 * 1024 * 1024
FFN_ROWS = 512
PROJ_ROWS = 512
ATTN_BLOCK = 1024
PAGE_GROUP = 16
PROJ_OFFSETS = (0, A_WIDTH, 2 * A_WIDTH, 2 * A_WIDTH + B_WIDTH, 2 * A_WIDTH + B_WIDTH + C_WIDTH,
                2 * A_WIDTH + B_WIDTH + 2 * C_WIDTH, 2 * A_WIDTH + B_WIDTH + 3 * C_WIDTH)


def _resident(shape):
    nd = len(shape)
    return pl.BlockSpec(shape, lambda *_: (0,) * nd, pipeline_mode=pl.Buffered(1))


def _layer_resident(shape, layer):
    nd = len(shape)
    return pl.BlockSpec((None,) + tuple(shape), lambda *_: (layer,) + (0,) * nd, pipeline_mode=pl.Buffered(1))


def _rms(x, g):
    return x * lax.rsqrt(jnp.mean(x * x, axis=-1, keepdims=True) + RMS_EPS) * g


def _split3(x):
    hi = x.astype(BF16)
    r = x - hi.astype(F32)
    mid = r.astype(BF16)
    lo = (r - mid.astype(F32)).astype(BF16)
    return hi, mid, lo


def _dot(a, b):
    return jnp.dot(a, b, preferred_element_type=F32)


def _dot_nt(a, b):
    return lax.dot_general(a, b, (((1,), (1,)), ((), ())), preferred_element_type=F32)


def _exact_rows_dot(x, ones_mat):
    rows = x.shape[0]
    hi, mid, lo = _split3(x)
    parts = jnp.concatenate([hi.astype(F32), mid.astype(F32), lo.astype(F32)], axis=0).astype(BF16)
    out = _dot(parts, ones_mat)
    return out[0:rows] + out[rows:2 * rows] + out[2 * rows:3 * rows]


def _half_ffn(x, gpre_ref, gpost_ref, wg_ref, wu_ref, wd_ref, f_chunks):
    hb = _rms(x, gpre_ref[...]).astype(BF16)
    f = None
    for c0, c1 in f_chunks:
        g = _dot(hb, wg_ref[:, c0:c1])
        u = _dot(hb, wu_ref[:, c0:c1])
        act = (g / (1.0 + jnp.exp(-g)) * u).astype(BF16)
        part = _dot(act, wd_ref[c0:c1, :])
        f = part if f is None else f + part
    return x + FFN_RES * _rms(f, gpost_ref[...])


def _ffn_kernel(x_ref, gpre_ref, gpost_ref, wg_ref, wu_ref, wd_ref, o_ref, *, f_chunks):
    o_ref[...] = _half_ffn(x_ref[...], gpre_ref, gpost_ref, wg_ref, wu_ref, wd_ref, f_chunks)


def _mix_ffn_kernel(x_ref, a_ref, b_ref, c_ref, wout_ref, gmix_ref, gpre_ref, gpost_ref, wg_ref, wu_ref,
                    wd_ref, o_ref, *, f_chunks):
    y = (_dot(a_ref[...], wout_ref[0:A_WIDTH, :])
         + _dot(b_ref[...], wout_ref[A_WIDTH:A_WIDTH + B_WIDTH, :])
         + _dot(c_ref[...], wout_ref[A_WIDTH + B_WIDTH:, :]))
    x = x_ref[...] + _rms(y, gmix_ref[...])
    o_ref[...] = _half_ffn(x, gpre_ref, gpost_ref, wg_ref, wu_ref, wd_ref, f_chunks)


def _ffn_chunks(f):
    half = pl.cdiv(f // 2, MXU_TILE) * MXU_TILE
    return ((0, half), (half, f)) if half < f else ((0, f),)


def _ffn(x, gpre, gpost, wg, wu, wd, layer, tm):
    n, d = x.shape
    f = wg.shape[-1]
    row = pl.BlockSpec((tm, d), lambda i: (i, 0))
    return pl.pallas_call(
        functools.partial(_ffn_kernel, f_chunks=_ffn_chunks(f)),
        grid=(n // tm,),
        in_specs=[row, _resident((1, d)), _resident((1, d)), _layer_resident((d, f), layer),
                  _layer_resident((d, f), layer), _layer_resident((f, d), layer)],
        out_specs=row,
        out_shape=jax.ShapeDtypeStruct((n, d), F32),
        compiler_params=pltpu.CompilerParams(dimension_semantics=("parallel",),
                                             vmem_limit_bytes=VMEM_LIMIT),
        name="half_ffn",
    )(x, gpre, gpost, wg, wu, wd)


def _mix_ffn(x, a, b, c, wout, gmix, gpre, gpost, wg, wu, wd, layer, tm):
    n, d = x.shape
    f = wg.shape[-1]
    row = lambda w: pl.BlockSpec((tm, w), lambda i: (i, 0))
    return pl.pallas_call(
        functools.partial(_mix_ffn_kernel, f_chunks=_ffn_chunks(f)),
        grid=(n // tm,),
        in_specs=[row(d), row(A_WIDTH), row(B_WIDTH), row(C_WIDTH), _layer_resident(wout.shape[1:], layer),
                  _resident((1, d)), _resident((1, d)), _resident((1, d)), _layer_resident((d, f), layer),
                  _layer_resident((d, f), layer), _layer_resident((f, d), layer)],
        out_specs=row(d),
        out_shape=jax.ShapeDtypeStruct((n, d), F32),
        compiler_params=pltpu.CompilerParams(dimension_semantics=("parallel",),
                                             vmem_limit_bytes=VMEM_LIMIT),
        name="mix_out_ffn",
    )(x, a, b, c, wout, gmix, gpre, gpost, wg, wu, wd)


def _split_proj(p):
    return tuple(p[:, lo:hi] for lo, hi in zip(PROJ_OFFSETS[:-1], PROJ_OFFSETS[1:]))


def _log_sigmoid(z):
    return jnp.minimum(z, 0.0) - jnp.log1p(jnp.exp(-jnp.abs(z)))


def _head_rms(va, cvg):
    r = lax.broadcasted_iota(jnp.int32, (A_WIDTH, A_WIDTH), 0) // HEAD_DIM
    c = lax.broadcasted_iota(jnp.int32, (A_WIDTH, A_WIDTH), 1) // HEAD_DIM
    same_head = jnp.where(r == c, 1.0, 0.0).astype(BF16)
    x2 = va * va
    x2h = x2.astype(BF16)
    x2l = (x2 - x2h.astype(F32)).astype(BF16)
    ms = (_dot(x2h, same_head) + _dot(x2l, same_head)) * (1.0 / HEAD_DIM)
    return va * lax.rsqrt(ms + RMS_EPS) * cvg


def _gate_rows(w_heads, vnb, bias):
    rows = vnb.shape[0]
    lane_head = lax.broadcasted_iota(jnp.int32, (rows, A_WIDTH), 1) // HEAD_DIM
    z = bias
    for h in range(A_HEADS):
        z = z + jnp.where(lane_head == h, _dot(w_heads[h], vnb), 0.0)
    return z


def _pool_delta(delayed, count, half):
    w_small, w_big = POOL_WINDOWS[2 * half], POOL_WINDOWS[2 * half + 1]
    cur = delayed(0)
    acc = cur
    for j in range(1, w_small):
        acc = acc + delayed(j)
    acc_small = acc
    for j in range(w_small, w_big):
        acc = acc + delayed(j)
    is_small = lax.broadcasted_iota(jnp.int32, cur.shape, 1) < HEAD_DIM
    mean = jnp.where(is_small, acc_small, acc) / jnp.where(is_small, count(w_small), count(w_big))
    return mean - cur


def _proj_kernel(x_ref, gpre_ref, win_ref, wfl_ref, fb_ref, cvg_ref, ws_ref, bmat_ref, pbd_ref,
                 pscale_ref, k_all_ref, v_all_ref,
                 a_ref, b_ref, qt_ref, k2_ref, kc_ref, vt_ref, kout_ref, vout_ref, logf_ref, crow_ref,
                 tail_ref, ext_ref, carry_ref, *, tm):
    del k_all_ref, v_all_ref
    i = pl.program_id(1)

    @pl.when(i == 0)
    def _():
        carry_ref[...] = jnp.zeros_like(carry_ref)
        ext_ref[0:POOL_CARRY, :] = jnp.zeros((POOL_CARRY, B_WIDTH), F32)

    hb = _rms(x_ref[...], gpre_ref[...]).astype(BF16)
    sub = tm // PROJ_SPLIT
    proj = [(_dot(hb[s * sub:(s + 1) * sub], win_ref[...]),
             _dot(hb[s * sub:(s + 1) * sub], wfl_ref[...]))
            for s in range(PROJ_SPLIT)]

    r = lax.broadcasted_iota(jnp.int32, (sub, sub), 0)
    c = lax.broadcasted_iota(jnp.int32, (sub, sub), 1)
    upto = jnp.where(r <= c, 1.0, 0.0).astype(BF16)
    rr = lax.broadcasted_iota(jnp.int32, (CHUNK, CHUNK), 0)
    cc = lax.broadcasted_iota(jnp.int32, (CHUNK, CHUNK), 1)
    w_heads = [jnp.where(cc <= rr, ws_ref[h], 0.0).astype(BF16) for h in range(A_HEADS)]

    for s, (p, fl) in enumerate(proj):
        r0 = s * sub
        rows = slice(r0, r0 + sub)
        u, va, pin, q, k, v = _split_proj(p)

        k2_ref[rows, :] = k.astype(BF16)
        for hp in range(HEAD_PAIRS):
            sl = slice(hp * LANES, (hp + 1) * LANES)
            qt_ref[0, hp, :, rows] = (q[:, sl] * (HEAD_DIM ** -0.5 * LOG2E)).T.astype(BF16)
            kout_ref[0, 0, hp, :, rows] = k[:, sl].T
            v_t = v[:, sl].T
            vout_ref[0, 0, hp, :, rows] = v_t
            vt_ref[0, hp, :, rows] = v_t.astype(BF16)

        lf_t = _log_sigmoid(fl + fb_ref[...]).T[0:C_HEADS, :]
        logf_ref[0, :, rows] = lf_t
        csum = _exact_rows_dot(lf_t, upto) + carry_ref[:, 0:1]
        carry_ref[...] = jnp.broadcast_to(csum[:, sub - 1:sub], carry_ref.shape)
        csum2 = csum * LOG2E
        crow_ref[0, :, rows] = csum2
        hi, mid, lo = _split3(csum2)
        cols = jnp.concatenate([hi.astype(F32), mid.astype(F32), lo.astype(F32),
                                jnp.zeros((LANES - 3 * C_HEADS, sub), F32)], axis=0)
        kc_ref[rows, :] = cols.T.astype(BF16)

        vn = _head_rms(va, cvg_ref[...])
        vnb = vn.astype(BF16)
        for ci in range(sub // CHUNK):
            rs = slice(ci * CHUNK, (ci + 1) * CHUNK)
            z = _gate_rows(w_heads, vnb[rs], bmat_ref[...])
            a_ref[r0 + ci * CHUNK:r0 + (ci + 1) * CHUNK, :] = (u[rs] * z).astype(BF16)

        ext_ref[POOL_CARRY + r0:POOL_CARRY + r0 + sub, :] = pin
        pos = i * tm + r0 + lax.broadcasted_iota(jnp.int32, (sub, LANES), 0)
        d_halves = []
        for half in range(2):
            ls = slice(half * LANES, (half + 1) * LANES)
            d_halves.append(_pool_delta(
                lambda j: ext_ref[pl.ds(POOL_CARRY + r0 - j, sub), ls],
                lambda w: jnp.minimum(pos + 1, w).astype(F32), half))
        d = jnp.concatenate(d_halves, axis=1).astype(BF16)
        b_ref[rows, :] = (_dot(d, pbd_ref[...]) * pscale_ref[...]).astype(BF16)

    tail = ext_ref[tm:tm + POOL_CARRY, :]
    tail_ref[0] = tail
    ext_ref[0:POOL_CARRY, :] = tail


def _proj_prompt(x, bsz, seq, gpre, win, wfl, fb, cvg, ws, bmat, pbd, pscale, k_all, v_all, layer, tm):
    n, d = x.shape
    nt = seq // tm
    dp = win.shape[1]
    row = lambda w: pl.BlockSpec((tm, w), lambda b, i: (b * nt + i, 0))
    pair_t = pl.BlockSpec((1, HEAD_PAIRS, LANES, tm), lambda b, i: (b, 0, 0, i))
    layer_pair_t = pl.BlockSpec((1, 1, HEAD_PAIRS, LANES, tm), lambda b, i: (layer, b, 0, 0, i))
    head_t = pl.BlockSpec((1, C_HEADS, tm), lambda b, i: (b, 0, i))
    pair_shape = (bsz, HEAD_PAIRS, LANES, seq)
    untouched = pl.BlockSpec(memory_space=pl.ANY)
    out_shapes = (
        jax.ShapeDtypeStruct((n, A_WIDTH), BF16),
        jax.ShapeDtypeStruct((n, B_WIDTH), BF16),
        jax.ShapeDtypeStruct(pair_shape, BF16),
        jax.ShapeDtypeStruct((n, C_WIDTH), BF16),
        jax.ShapeDtypeStruct((n, LANES), BF16),
        jax.ShapeDtypeStruct(pair_shape, BF16),
        jax.ShapeDtypeStruct(k_all.shape, F32),
        jax.ShapeDtypeStruct(v_all.shape, F32),
        jax.ShapeDtypeStruct((bsz, C_HEADS, seq), F32),
        jax.ShapeDtypeStruct((bsz, C_HEADS, seq), F32),
        jax.ShapeDtypeStruct((bsz, POOL_CARRY, B_WIDTH), F32),
    )
    out_specs = (
        row(A_WIDTH), row(B_WIDTH), pair_t, row(C_WIDTH), row(LANES), pair_t, layer_pair_t, layer_pair_t,
        head_t, head_t,
        pl.BlockSpec((1, POOL_CARRY, B_WIDTH), lambda b, i: (b, 0, 0)),
    )
    return pl.pallas_call(
        functools.partial(_proj_kernel, tm=tm),
        grid=(bsz, nt),
        in_specs=[row(d), _resident((1, d)), _resident((d, dp)), _resident((d, LANES)),
                  _resident((1, LANES)), _resident((1, A_WIDTH)), _resident((A_HEADS, CHUNK, CHUNK)),
                  _resident((CHUNK, A_WIDTH)), _resident((B_WIDTH, B_WIDTH)), _resident((1, B_WIDTH)),
                  untouched, untouched],
        out_specs=out_specs,
        out_shape=out_shapes,
        input_output_aliases={10: 6, 11: 7},
        scratch_shapes=[pltpu.VMEM((tm + POOL_CARRY, B_WIDTH), F32), pltpu.VMEM((C_HEADS, LANES), F32)],
        compiler_params=pltpu.CompilerParams(dimension_semantics=("arbitrary", "arbitrary"),
                                             vmem_limit_bytes=VMEM_LIMIT),
        name="proj_mix",
    )(x, gpre, win, wfl, fb, cvg, ws, bmat, pbd, pscale, k_all, v_all)


def _attn_kernel(qt_ref, k2_ref, kc_ref, vt_ref, cq_ref, o_ref, *, blk):
    hp = pl.program_id(1)
    qi = pl.program_id(2)
    q2 = qt_ref[0, 0]
    row = lax.broadcasted_iota(jnp.int32, (LANES, blk), 0)
    qa, cq = [], []
    for hh in range(2):
        own_q = jnp.where((row // HEAD_DIM) == hh, q2, jnp.zeros_like(q2))
        minus_ck = jnp.where((row < 3 * C_HEADS) & ((row % C_HEADS) == 2 * hp + hh), -1.0, 0.0)
        qa.append(jnp.concatenate([own_q, minus_ck.astype(BF16)], axis=0))
        cq.append(cq_ref[0, 0, hh:hh + 1, :])
    ones = jnp.ones((DENOM_ROWS, blk), BF16)

    def load(j):
        off = pl.multiple_of(j * blk, blk)
        kk = jnp.concatenate([k2_ref[pl.ds(off, blk), :], kc_ref[pl.ds(off, blk), :]], axis=1)
        va = [jnp.concatenate([vt_ref[0, 0, hh * HEAD_DIM:(hh + 1) * HEAD_DIM, pl.ds(off, blk)], ones], axis=0)
              for hh in range(2)]
        return kk, va

    def exact(j, carry, masked):
        kk, va = load(j)
        ts = [_dot(kk, qa[hh]) for hh in range(2)]
        new = []
        for hh in range(2):
            m_run, acc = carry[hh]
            t = ts[hh]
            if masked:
                kpos = lax.broadcasted_iota(jnp.int32, (blk, blk), 0)
                qpos = lax.broadcasted_iota(jnp.int32, (blk, blk), 1)
                t = jnp.where(kpos <= qpos, t, NEG_BIG)
            m_new = jnp.maximum(m_run, jnp.max(t, axis=0, keepdims=True) + cq[hh])
            pr = jnp.exp2(t - (m_new - cq[hh])).astype(BF16)
            new.append((m_new, jnp.exp2(m_run - m_new) * acc + _dot(va[hh], pr)))
        return tuple(new)

    init = tuple((jnp.full((1, blk), NEG_BIG, F32), jnp.zeros((HEAD_DIM + DENOM_ROWS, blk), F32))
                 for _ in range(2))

    def diagonal():
        kk, va = load(qi)
        strips = [(hh, s * DIAG_TILE, (s + 1) * DIAG_TILE) for hh in range(2) for s in range(blk // DIAG_TILE)]
        t_diag = [_dot(kk[lo:hi], qa[hh][:, lo:hi]) for hh, lo, hi in strips]
        t_old = [_dot(kk[0:lo], qa[hh][:, lo:hi]) if lo else None for hh, lo, hi in strips]
        kpos = lax.broadcasted_iota(jnp.int32, (DIAG_TILE, DIAG_TILE), 0)
        qpos = lax.broadcasted_iota(jnp.int32, (DIAG_TILE, DIAG_TILE), 1)
        m_tile, m_fin, p_diag, p_old, rise = [], [], [], [], []
        for (hh, lo, hi), td, to in zip(strips, t_diag, t_old):
            cq_s = cq[hh][:, lo:hi]
            td = jnp.where(kpos <= qpos, td, NEG_BIG)
            m_s = jnp.max(td, axis=0, keepdims=True) + cq_s
            p_diag.append(jnp.exp2(td - (m_s - cq_s)).astype(BF16))
            m_tile.append(m_s)
            if to is None:
                p_old.append(None)
                m_fin.append(m_s)
            else:
                p_old.append(jnp.exp2(to - (m_s - cq_s)).astype(BF16))
                m_fin.append(jnp.maximum(m_s, jnp.max(to, axis=0, keepdims=True) + cq_s))
                rise.append(m_fin[-1] - m_s)
        accs = []
        for (hh, lo, hi), pd, po, m_s, m_f in zip(strips, p_diag, p_old, m_tile, m_fin):
            acc_s = _dot(va[hh][:, lo:hi], pd)
            if po is not None:
                acc_s = jnp.exp2(m_s - m_f) * (acc_s + _dot(va[hh][:, 0:lo], po))
            accs.append(acc_s)
        per_head = blk // DIAG_TILE
        new = tuple((jnp.concatenate(m_fin[hh * per_head:(hh + 1) * per_head], axis=1),
                     jnp.concatenate(accs[hh * per_head:(hh + 1) * per_head], axis=1)) for hh in range(2))
        ok = jnp.all(jnp.concatenate(rise, axis=1) <= STALE_MAX_SLACK)
        return lax.cond(ok, lambda: new, lambda: exact(qi, init, True))

    carry = diagonal()

    def older(i, carry):
        j = qi - 1 - i
        kk, va = load(j)
        new, rise = [], []
        for hh in range(2):
            m_run, acc = carry[hh]
            t = _dot(kk, qa[hh])
            pr = jnp.exp2(t - (m_run - cq[hh])).astype(BF16)
            m_new = jnp.maximum(m_run, jnp.max(t, axis=0, keepdims=True) + cq[hh])
            new.append((m_new, jnp.exp2(m_run - m_new) * (acc + _dot(va[hh], pr))))
            rise.append(m_new - m_run)
        ok = jnp.all(jnp.concatenate(rise, axis=1) <= STALE_MAX_SLACK)
        return lax.cond(ok, lambda: tuple(new), lambda: exact(j, carry, False))

    carry = lax.fori_loop(0, qi, older, carry)
    out_t = jnp.concatenate([acc[0:HEAD_DIM] / acc[HEAD_DIM:HEAD_DIM + 1] for _, acc in carry], axis=0)
    o_ref[...] = out_t.T.astype(BF16)


def _attn_prompt(qt, k2, kc, vt, crow, bsz, seq, blk):
    n = bsz * seq
    nq = seq // blk
    crow4 = crow.reshape(bsz, HEAD_PAIRS, 2, seq)
    return pl.pallas_call(
        functools.partial(_attn_kernel, blk=blk),
        grid=(bsz, HEAD_PAIRS, nq),
        in_specs=[
            pl.BlockSpec((1, 1, LANES, blk), lambda b, hp, qi: (b, hp, 0, qi)),
            pl.BlockSpec((seq, LANES), lambda b, hp, qi: (b, hp)),
            pl.BlockSpec((seq, LANES), lambda b, hp, qi: (b, 0)),
            pl.BlockSpec((1, 1, LANES, seq), lambda b, hp, qi: (b, hp, 0, 0)),
            pl.BlockSpec((1, 1, 2, blk), lambda b, hp, qi: (b, hp, 0, qi)),
        ],
        out_specs=pl.BlockSpec((blk, LANES), lambda b, hp, qi: (b * nq + qi, hp)),
        out_shape=jax.ShapeDtypeStruct((n, C_WIDTH), BF16),
        compiler_params=pltpu.CompilerParams(
            dimension_semantics=("arbitrary", "arbitrary", "arbitrary"), vmem_limit_bytes=VMEM_LIMIT),
        name="forget_attn",
    )(qt, k2, kc, vt, crow4)


def _sproj_kernel(x_ref, gpre_ref, win_ref, wfl_ref, fb_ref, cvg_ref, wexp_ref, bsamp_ref, pbd_ref,
                  pscale_ref, state_ref,
                  a_ref, b_ref, q_ref, k_ref, v_ref, logf_ref, cnew_ref, vn_ref, pin_ref,
                  *, db, n_new, past):
    rows = db * n_new
    hb = _rms(x_ref[...], gpre_ref[...]).astype(BF16)
    p = _dot(hb, win_ref[...])
    fl = _dot(hb, wfl_ref[...])
    u, va, pin, q_ref[...], k_ref[...], v_ref[...] = _split_proj(p)
    pin_ref[...] = pin

    lf = _log_sigmoid(fl + fb_ref[...])
    logf_ref[...] = lf[:, 0:C_HEADS]
    run = lf[0:db]
    cnew_ref[0:db, :] = run[:, 0:C_HEADS]
    for t in range(1, n_new):
        run = run + lf[t * db:(t + 1) * db]
        cnew_ref[t * db:(t + 1) * db, :] = run[:, 0:C_HEADS]

    vn = _head_rms(va, cvg_ref[...])
    vn_ref[...] = vn
    rr = lax.broadcasted_iota(jnp.int32, (rows, rows), 0)
    cc = lax.broadcasted_iota(jnp.int32, (rows, rows), 1)
    keep = ((rr % db) == (cc % db)) & ((cc // db) <= (rr // db))
    w_heads = [jnp.where(keep, wexp_ref[h], 0.0).astype(BF16) for h in range(A_HEADS)]
    z = _gate_rows(w_heads, vn.astype(BF16), bsamp_ref[...])
    a_ref[...] = (u * z).astype(BF16)

    def ext(idx, ls):
        if idx < POOL_BUF:
            return state_ref[idx][:, ls]
        t = idx - POOL_BUF
        return pin[t * db:(t + 1) * db, ls]

    d_rows = []
    for t in range(n_new):
        d_halves = []
        for half in range(2):
            ls = slice(half * LANES, (half + 1) * LANES)
            d_halves.append(_pool_delta(lambda j: ext(POOL_BUF + t - j, ls),
                                        lambda w: float(min(past + t + 1, w)), half))
        d_rows.append(jnp.concatenate(d_halves, axis=1))
    d = jnp.concatenate(d_rows, axis=0).astype(BF16)
    b_ref[...] = (_dot(d, pbd_ref[...]) * pscale_ref[...]).astype(BF16)


def _proj_sample(x, gpre, win, wfl, fb, cvg, wexp, bsamp, pbd, pscale, state_t, db, n_new, past):
    rows, d = x.shape
    out_shapes = (
        jax.ShapeDtypeStruct((rows, A_WIDTH), BF16),
        jax.ShapeDtypeStruct((rows, B_WIDTH), BF16),
        jax.ShapeDtypeStruct((rows, C_WIDTH), F32),
        jax.ShapeDtypeStruct((rows, C_WIDTH), F32),
        jax.ShapeDtypeStruct((rows, C_WIDTH), F32),
        jax.ShapeDtypeStruct((rows, C_HEADS), F32),
        jax.ShapeDtypeStruct((rows, C_HEADS), F32),
        jax.ShapeDtypeStruct((rows, A_WIDTH), F32),
        jax.ShapeDtypeStruct((rows, B_WIDTH), F32),
    )
    args = (x, gpre, win, wfl, fb, cvg, wexp, bsamp, pbd, pscale, state_t)
    return pl.pallas_call(
        functools.partial(_sproj_kernel, db=db, n_new=n_new, past=past),
        grid=(1,),
        in_specs=[pl.BlockSpec(a.shape, lambda i, nd=a.ndim: (0,) * nd) for a in args],
        out_specs=tuple(pl.BlockSpec(s.shape, lambda i: (0, 0)) for s in out_shapes),
        out_shape=out_shapes,
        compiler_params=pltpu.CompilerParams(dimension_semantics=("arbitrary",),
                                             vmem_limit_bytes=VMEM_LIMIT),
        name="proj_mix_sample",
    )(*args)


def _pattn_kernel(pt_ref, q_ref, knew_ref, vnew_ref, cq_ref, cknew_ref, kt_hbm, vt_hbm, lft_hbm,
                  o_ref, kbuf, vbuf, lfbuf, sems, *, layer, group, n_pages, n_new):
    b = pl.program_id(0)
    n_seq = pl.num_programs(0)
    n_steps = n_pages // group
    rows = n_new * C_HEADS
    page = lfbuf.shape[-1]

    def copies(seq, step, slot):
        out = []
        for i in range(group):
            phys = pt_ref[seq, n_pages - 1 - (step * group + i)]
            out.append(pltpu.make_async_copy(kt_hbm.at[layer, phys], kbuf.at[slot, i], sems.at[slot, 0]))
            out.append(pltpu.make_async_copy(vt_hbm.at[layer, phys], vbuf.at[slot, i], sems.at[slot, 1]))
            out.append(pltpu.make_async_copy(lft_hbm.at[layer, phys], lfbuf.at[slot, i], sems.at[slot, 2]))
        return out

    @pl.when(b == 0)
    def _():
        for cp in copies(0, 0, 0):
            cp.start()

    row_head = lax.broadcasted_iota(jnp.int32, (rows, C_WIDTH), 0) % C_HEADS
    lane_head = lax.broadcasted_iota(jnp.int32, (rows, C_WIDTH), 1) // HEAD_DIM
    own = row_head == lane_head
    qs = q_ref[0] * (HEAD_DIM ** -0.5)
    qrep = jnp.concatenate([jnp.broadcast_to(qs[t:t + 1], (C_HEADS, C_WIDTH)) for t in range(n_new)], axis=0)
    qbd = jnp.where(own, qrep, 0.0).astype(BF16)
    cq = cq_ref[0]
    sr = lax.broadcasted_iota(jnp.int32, (page, 2 * page), 0)
    sc = lax.broadcasted_iota(jnp.int32, (page, 2 * page), 1)
    later_or_all = jnp.where((sr > sc) | (sc >= page), 1.0, 0.0).astype(BF16)

    def update(state, scores, pv):
        m_run, l_run, acc = state
        m_cur = m_run
        for s in scores:
            m_cur = jnp.maximum(m_cur, jnp.max(s, axis=1, keepdims=True))
        alpha = jnp.exp(m_run - m_cur)
        l_new = alpha * l_run
        acc = alpha * acc
        for i, s in enumerate(scores):
            pr = jnp.exp(s - m_cur)
            l_new = l_new + jnp.sum(pr, axis=1, keepdims=True)
            acc = acc + pv(i, pr.astype(BF16))
        return m_cur, l_new, acc

    def step(s, carry):
        state, later = carry
        slot = (b * n_steps + s) % 2

        @pl.when(s + 1 < n_steps)
        def _():
            for cp in copies(b, s + 1, 1 - slot):
                cp.start()

        @pl.when((s + 1 == n_steps) & (b + 1 < n_seq))
        def _():
            for cp in copies(b + 1, 0, 1 - slot):
                cp.start()

        for cp in copies(b, s, slot):
            cp.wait()

        sums = _exact_rows_dot(jnp.concatenate([lfbuf[slot, i] for i in range(group)], axis=0), later_or_all)
        scores = []
        for i in range(group):
            hs = slice(i * C_HEADS, (i + 1) * C_HEADS)
            bias = sums[hs, 0:page] + later
            later = later + sums[hs, page:2 * page]
            sco = _dot(qbd, kbuf[slot, i].astype(BF16))
            scores.append(sco + jnp.concatenate([bias] * n_new, axis=0) + cq)
        state = update(state, scores, lambda i, pr: _dot_nt(pr, vbuf[slot, i].astype(BF16)))
        return state, later

    init = ((jnp.full((rows, 1), NEG_BIG, F32), jnp.zeros((rows, 1), F32), jnp.zeros((rows, C_WIDTH), F32)),
            jnp.zeros((C_HEADS, page), F32))
    state, _ = lax.fori_loop(0, n_steps, step, init)

    s_new = _dot_nt(qbd, knew_ref[0].astype(BF16))
    s_new = s_new + cq[:, 0:NEW_PAD] - cknew_ref[0]
    tq = lax.broadcasted_iota(jnp.int32, (rows, NEW_PAD), 0) // C_HEADS
    sk = lax.broadcasted_iota(jnp.int32, (rows, NEW_PAD), 1)
    s_new = jnp.where(sk <= tq, s_new, NEG_BIG)
    _, l_fin, acc = update(state, [s_new], lambda i, pr: _dot(pr, vnew_ref[0].astype(BF16)))
    full = jnp.where(own, acc / l_fin, 0.0)
    o_ref[0] = jnp.sum(full.reshape(n_new, C_HEADS, C_WIDTH), axis=1).astype(BF16)


def _paged_attn(page_table, q_b, knew_b, vnew_b, cq_rep, cknew, cache_kt, cache_vt, cache_lft, layer, group):
    db, n_pages = page_table.shape
    n_new = q_b.shape[1]
    page = cache_kt.shape[-1]
    rows = n_new * C_HEADS
    per_seq = lambda shape: pl.BlockSpec((1,) + shape, lambda b, pt: (b,) + (0,) * len(shape))
    in_hbm = pl.BlockSpec(memory_space=pl.ANY)
    grid_spec = pltpu.PrefetchScalarGridSpec(
        num_scalar_prefetch=1,
        grid=(db,),
        in_specs=[per_seq((n_new, C_WIDTH)), per_seq(knew_b.shape[1:]), per_seq(vnew_b.shape[1:]),
                  per_seq((rows, LANES)), per_seq(cknew.shape[1:]), in_hbm, in_hbm, in_hbm],
        out_specs=pl.BlockSpec((1, n_new, C_WIDTH), lambda b, pt: (b, 0, 0)),
        scratch_shapes=[pltpu.VMEM((2, group, C_WIDTH, page), F32), pltpu.VMEM((2, group, C_WIDTH, page), F32),
                        pltpu.VMEM((2, group, C_HEADS, page), F32), pltpu.SemaphoreType.DMA((2, 3))],
    )
    return pl.pallas_call(
        functools.partial(_pattn_kernel, layer=layer, group=group, n_pages=n_pages, n_new=n_new),
        grid_spec=grid_spec,
        out_shape=jax.ShapeDtypeStruct((db, n_new, C_WIDTH), BF16),
        compiler_params=pltpu.CompilerParams(dimension_semantics=("arbitrary",),
                                             vmem_limit_bytes=VMEM_LIMIT),
        name="paged_forget_attn",
    )(page_table, q_b, knew_b, vnew_b, cq_rep, cknew, cache_kt, cache_vt, cache_lft)


def _row_tile(n, target):
    t = min(n, target)
    while n % t:
        t //= 2
    return t


def kernel(x_prompt, x_sample, cache_k, cache_v, cache_logf, state_pool, page_table,
           ffn1_g_pre, ffn1_g_post, ffn1_w_gate, ffn1_w_up, ffn1_w_down,
           mix_g_pre, mix_g_post, w_in, w_out,
           chunk_v_g, chunk_w_s, chunk_b, pool_w, pool_scale, forget_b,
           ffn2_g_pre, ffn2_g_post, ffn2_w_gate, ffn2_w_up, ffn2_w_down):
    bsz, seq, d = x_prompt.shape
    db, n_new, _ = x_sample.shape
    depth, n_phys, page = cache_k.shape[:3]
    n_pages = page_table.shape[1]
    past = n_pages * page
    n = bsz * seq
    rows_s = db * n_new
    d_main = w_in.shape[-1] - C_HEADS

    tm_ffn = _row_tile(n, FFN_ROWS)
    tm_proj = _row_tile(seq, PROJ_ROWS)
    blk = _row_tile(seq, ATTN_BLOCK)
    group = _row_tile(n_pages, PAGE_GROUP)

    xp = x_prompt.reshape(n, d)
    xs = jnp.transpose(x_sample, (1, 0, 2)).reshape(rows_s, d)
    cache_kt = jnp.transpose(cache_k, (0, 1, 3, 4, 2)).reshape(depth, n_phys, C_WIDTH, page)
    cache_vt = jnp.transpose(cache_v, (0, 1, 3, 4, 2)).reshape(depth, n_phys, C_WIDTH, page)
    cache_lft = jnp.transpose(cache_logf, (0, 1, 3, 2))

    outs = {k: [] for k in ("lfp", "poolp", "ks", "vs", "lfs", "pools", "cvs")}
    k_all = jnp.zeros((depth, bsz, HEAD_PAIRS, LANES, seq), F32)
    v_all = jnp.zeros((depth, bsz, HEAD_PAIRS, LANES, seq), F32)
    ffn1_w = tuple(w.astype(BF16) for w in (ffn1_w_gate, ffn1_w_up, ffn1_w_down))
    ffn2_w = tuple(w.astype(BF16) for w in (ffn2_w_gate, ffn2_w_up, ffn2_w_down))
    wout = w_out.astype(BF16)
    for l in range(depth):
        row2 = lambda v: v[l].reshape(1, -1)
        f1 = (row2(ffn1_g_pre), row2(ffn1_g_post)) + ffn1_w + (l,)
        f2 = (row2(ffn2_g_pre), row2(ffn2_g_post)) + ffn2_w + (l,)
        win = w_in[l, :, :d_main].astype(BF16)
        wfl = jnp.pad(w_in[l, :, d_main:], ((0, 0), (0, LANES - C_HEADS))).astype(BF16)
        fb = jnp.pad(forget_b[l], (0, LANES - C_HEADS)).reshape(1, LANES)
        bmat = jnp.repeat(chunk_b[l].T, HEAD_DIM, axis=1)
        pbd = jax.scipy.linalg.block_diag(*[pool_w[l, g] for g in range(len(POOL_WINDOWS))]).astype(BF16)
        mixp = (row2(mix_g_pre), win, wfl, fb, row2(chunk_v_g))
        pool_p = (pbd, row2(pool_scale))

        xp = _ffn(xp, *f1, tm_ffn)
        (a, b, qt, k2, kc, vt, k_all, v_all, lf_t, crow, tail) = _proj_prompt(
            xp, bsz, seq, *mixp, chunk_w_s[l], bmat, *pool_p, k_all, v_all, l, tm_proj)
        c = _attn_prompt(qt, k2, kc, vt, crow, bsz, seq, blk)
        xp = _mix_ffn(xp, a, b, c, wout, row2(mix_g_post), *f2, tm_ffn)
        outs["lfp"].append(jnp.transpose(lf_t, (0, 2, 1)))
        outs["poolp"].append(tail[:, POOL_CARRY - POOL_BUF:])

        xs = _ffn(xs, *f1, rows_s)
        wexp = jnp.repeat(jnp.repeat(chunk_w_s[l, :, :n_new, :n_new], db, axis=1), db, axis=2)
        bsamp = jnp.repeat(bmat[:n_new], db, axis=0)
        state_t = jnp.transpose(state_pool[l], (1, 0, 2))
        (a_s, b_s, q_s, k_s, v_s, lf_s, cnew, vn_s, pin_s) = _proj_sample(
            xs, *mixp, wexp, bsamp, *pool_p, state_t, db, n_new, past)
        to_b = lambda v: jnp.transpose(v.reshape(n_new, db, -1), (1, 0, 2))
        q_b, k_b, v_b, cnew_b = to_b(q_s), to_b(k_s), to_b(v_s), to_b(cnew)
        pad_new = ((0, 0), (0, NEW_PAD - n_new), (0, 0))
        cq_rep = jnp.broadcast_to(cnew_b.reshape(db, n_new * C_HEADS, 1), (db, n_new * C_HEADS, LANES))
        cknew = jnp.pad(jnp.tile(jnp.transpose(cnew_b, (0, 2, 1)), (1, n_new, 1)),
                        ((0, 0), (0, 0), (0, NEW_PAD - n_new)))
        c_s = _paged_attn(page_table, q_b, jnp.pad(k_b, pad_new), jnp.pad(v_b, pad_new), cq_rep, cknew,
                          cache_kt, cache_vt, cache_lft, l, group)
        c_s = jnp.transpose(c_s, (1, 0, 2)).reshape(rows_s, C_WIDTH)
        xs = _mix_ffn(xs, a_s, b_s, c_s, wout, row2(mix_g_post), *f2, rows_s)
        outs["ks"].append(k_b.reshape(db, n_new, C_HEADS, HEAD_DIM))
        outs["vs"].append(v_b.reshape(db, n_new, C_HEADS, HEAD_DIM))
        outs["lfs"].append(to_b(lf_s))
        outs["pools"].append(jnp.concatenate([state_pool[l], to_b(pin_s)], axis=1)[:, -POOL_BUF:])
        outs["cvs"].append(to_b(vn_s))

    y_prompt = xp.reshape(bsz, seq, d)
    y_sample = jnp.transpose(xs.reshape(n_new, db, d), (1, 0, 2))
    st = lambda key: jnp.stack(outs[key])
    to_heads = lambda t: jnp.transpose(t.reshape(depth, bsz, C_HEADS, HEAD_DIM, seq), (0, 1, 4, 2, 3))
    return (y_prompt, y_sample, to_heads(k_all), to_heads(v_all), st("lfp"), st("poolp"),
            st("ks"), st("vs"), st("lfs"), st("pools"), st("cvs"))
```

```python
import functools

import jax
import jax.numpy as jnp
from jax import lax
from jax.experimental import pallas as pl
from jax.experimental.pallas import tpu as pltpu

F32 = jnp.float32
BF16 = jnp.bfloat16

RMS_EPS = 1e-6
FFN_RES = 0.5
HEAD_DIM = 64
CHUNK = 128
POOL_WINDOWS = (2, 4, 8, 16)
POOL_BUF = max(POOL_WINDOWS) - 1
POOL_CARRY = POOL_BUF + 1
A_WIDTH = 256
A_HEADS = A_WIDTH // HEAD_DIM
B_WIDTH = 256
C_HEADS = 8
C_WIDTH = C_HEADS * HEAD_DIM
LANES = 128
MXU_TILE = 256
BF16_SUBLANES = 16
HEAD_PAIRS = C_WIDTH // LANES
NEW_PAD = BF16_SUBLANES
DENOM_ROWS = BF16_SUBLANES
LOG2E = 1.4426950408889634
PROJ_SPLIT = 2
DIAG_TILE = MXU_TILE
STALE_MAX_SLACK = 16.0
NEG_BIG = -1e30
VMEM_LIMIT = 56 * 1024 * 1024
FFN_ROWS = 1024
PROJ_ROWS = 512
ATTN_BLOCK = 1024
PAGE_GROUP = 32
PROJ_OFFSETS = (0, A_WIDTH, 2 * A_WIDTH, 2 * A_WIDTH + B_WIDTH, 2 * A_WIDTH + B_WIDTH + C_WIDTH,
                2 * A_WIDTH + B_WIDTH + 2 * C_WIDTH, 2 * A_WIDTH + B_WIDTH + 3 * C_WIDTH)


def _resident(shape):
    nd = len(shape)
    return pl.BlockSpec(shape, lambda *_: (0,) * nd, pipeline_mode=pl.Buffered(1))


def _layer_resident(shape, layer):
    nd = len(shape)
    return pl.BlockSpec((None,) + tuple(shape), lambda *_: (layer,) + (0,) * nd, pipeline_mode=pl.Buffered(1))


def _rms(x, g):
    return x * lax.rsqrt(jnp.mean(x * x, axis=-1, keepdims=True) + RMS_EPS) * g


def _split3(x):
    hi = x.astype(BF16)
    r = x - hi.astype(F32)
    mid = r.astype(BF16)
    lo = (r - mid.astype(F32)).astype(BF16)
    return hi, mid, lo


def _dot(a, b):
    return jnp.dot(a, b, preferred_element_type=F32)


def _dot_nt(a, b):
    return lax.dot_general(a, b, (((1,), (1,)), ((), ())), preferred_element_type=F32)


def _exact_rows_dot(x, ones_mat):
    rows = x.shape[0]
    hi, mid, lo = _split3(x)
    parts = jnp.concatenate([hi.astype(F32), mid.astype(F32), lo.astype(F32)], axis=0).astype(BF16)
    out = _dot(parts, ones_mat)
    return out[0:rows] + out[rows:2 * rows] + out[2 * rows:3 * rows]


def _half_ffn(x, gpre_ref, gpost_ref, wg_ref, wu_ref, wd_ref, f_chunks):
    hb = _rms(x, gpre_ref[...]).astype(BF16)
    f = None
    for c0, c1 in f_chunks:
        g = _dot(hb, wg_ref[:, c0:c1])
        u = _dot(hb, wu_ref[:, c0:c1])
        act = (g / (1.0 + jnp.exp(-g)) * u).astype(BF16)
        part = _dot(act, wd_ref[c0:c1, :])
        f = part if f is None else f + part
    return x + FFN_RES * _rms(f, gpost_ref[...])


def _ffn_kernel(x_ref, gpre_ref, gpost_ref, wg_ref, wu_ref, wd_ref, o_ref, *, f_chunks):
    o_ref[...] = _half_ffn(x_ref[...], gpre_ref, gpost_ref, wg_ref, wu_ref, wd_ref, f_chunks)


def _mix_ffn_kernel(x_ref, a_ref, b_ref, c_ref, wout_ref, gmix_ref, gpre_ref, gpost_ref, wg_ref, wu_ref,
                    wd_ref, o_ref, *, f_chunks):
    y = (_dot(a_ref[...], wout_ref[0:A_WIDTH, :])
         + _dot(b_ref[...], wout_ref[A_WIDTH:A_WIDTH + B_WIDTH, :])
         + _dot(c_ref[...], wout_ref[A_WIDTH + B_WIDTH:, :]))
    x = x_ref[...] + _rms(y, gmix_ref[...])
    o_ref[...] = _half_ffn(x, gpre_ref, gpost_ref, wg_ref, wu_ref, wd_ref, f_chunks)


def _ffn_chunks(f):
    half = pl.cdiv(f // 2, MXU_TILE) * MXU_TILE
    return ((0, half), (half, f)) if half < f else ((0, f),)


def _ffn(x, gpre, gpost, wg, wu, wd, layer, tm):
    n, d = x.shape
    f = wg.shape[-1]
    row = pl.BlockSpec((tm, d), lambda i: (i, 0))
    return pl.pallas_call(
        functools.partial(_ffn_kernel, f_chunks=_ffn_chunks(f)),
        grid=(n // tm,),
        in_specs=[row, _resident((1, d)), _resident((1, d)), _layer_resident((d, f), layer),
                  _layer_resident((d, f), layer), _layer_resident((f, d), layer)],
        out_specs=row,
        out_shape=jax.ShapeDtypeStruct((n, d), F32),
        compiler_params=pltpu.CompilerParams(dimension_semantics=("parallel",),
                                             vmem_limit_bytes=VMEM_LIMIT),
        name="half_ffn",
    )(x, gpre, gpost, wg, wu, wd)


def _mix_ffn(x, a, b, c, wout, gmix, gpre, gpost, wg, wu, wd, layer, tm):
    n, d = x.shape
    f = wg.shape[-1]
    row = lambda w: pl.BlockSpec((tm, w), lambda i: (i, 0))
    return pl.pallas_call(
        functools.partial(_mix_ffn_kernel, f_chunks=_ffn_chunks(f)),
        grid=(n // tm,),
        in_specs=[row(d), row(A_WIDTH), row(B_WIDTH), row(C_WIDTH), _layer_resident(wout.shape[1:], layer),
                  _resident((1, d)), _resident((1, d)), _resident((1, d)), _layer_resident((d, f), layer),
                  _layer_resident((d, f), layer), _layer_resident((f, d), layer)],
        out_specs=row(d),
        out_shape=jax.ShapeDtypeStruct((n, d), F32),
        compiler_params=pltpu.CompilerParams(dimension_semantics=("parallel",),
                                             vmem_limit_bytes=VMEM_LIMIT),
        name="mix_out_ffn",
    )(x, a, b, c, wout, gmix, gpre, gpost, wg, wu, wd)


def _split_proj(p):
    return tuple(p[:, lo:hi] for lo, hi in zip(PROJ_OFFSETS[:-1], PROJ_OFFSETS[1:]))


def _log_sigmoid(z):
    return jnp.minimum(z, 0.0) - jnp.log1p(jnp.exp(-jnp.abs(z)))


def _head_rms(va, cvg):
    r = lax.broadcasted_iota(jnp.int32, (A_WIDTH, A_WIDTH), 0) // HEAD_DIM
    c = lax.broadcasted_iota(jnp.int32, (A_WIDTH, A_WIDTH), 1) // HEAD_DIM
    same_head = jnp.where(r == c, 1.0, 0.0).astype(BF16)
    x2 = va * va
    x2h = x2.astype(BF16)
    x2l = (x2 - x2h.astype(F32)).astype(BF16)
    ms = (_dot(x2h, same_head) + _dot(x2l, same_head)) * (1.0 / HEAD_DIM)
    return va * lax.rsqrt(ms + RMS_EPS) * cvg


def _gate_rows(w_heads, vnb, bias):
    rows = vnb.shape[0]
    lane_head = lax.broadcasted_iota(jnp.int32, (rows, A_WIDTH), 1) // HEAD_DIM
    z = bias
    for h in range(A_HEADS):
        z = z + jnp.where(lane_head == h, _dot(w_heads[h], vnb), 0.0)
    return z


def _pool_delta(delayed, count, half):
    w_small, w_big = POOL_WINDOWS[2 * half], POOL_WINDOWS[2 * half + 1]
    cur = delayed(0)
    acc = cur
    for j in range(1, w_small):
        acc = acc + delayed(j)
    acc_small = acc
    for j in range(w_small, w_big):
        acc = acc + delayed(j)
    is_small = lax.broadcasted_iota(jnp.int32, cur.shape, 1) < HEAD_DIM
    mean = jnp.where(is_small, acc_small, acc) / jnp.where(is_small, count(w_small), count(w_big))
    return mean - cur


def _proj_kernel(x_ref, gpre_ref, win_ref, wfl_ref, fb_ref, cvg_ref, ws_ref, bmat_ref, pbd_ref,
                 pscale_ref, k_all_ref, v_all_ref,
                 a_ref, b_ref, qt_ref, k2_ref, kc_ref, vt_ref, kout_ref, vout_ref, logf_ref, crow_ref,
                 tail_ref, ext_ref, carry_ref, *, tm):
    del k_all_ref, v_all_ref
    i = pl.program_id(1)

    @pl.when(i == 0)
    def _():
        carry_ref[...] = jnp.zeros_like(carry_ref)
        ext_ref[0:POOL_CARRY, :] = jnp.zeros((POOL_CARRY, B_WIDTH), F32)

    hb = _rms(x_ref[...], gpre_ref[...]).astype(BF16)
    sub = tm // PROJ_SPLIT
    proj = [(_dot(hb[s * sub:(s + 1) * sub], win_ref[...]),
             _dot(hb[s * sub:(s + 1) * sub], wfl_ref[...]))
            for s in range(PROJ_SPLIT)]

    r = lax.broadcasted_iota(jnp.int32, (sub, sub), 0)
    c = lax.broadcasted_iota(jnp.int32, (sub, sub), 1)
    upto = jnp.where(r <= c, 1.0, 0.0).astype(BF16)
    rr = lax.broadcasted_iota(jnp.int32, (CHUNK, CHUNK), 0)
    cc = lax.broadcasted_iota(jnp.int32, (CHUNK, CHUNK), 1)
    w_heads = [jnp.where(cc <= rr, ws_ref[h], 0.0).astype(BF16) for h in range(A_HEADS)]

    for s, (p, fl) in enumerate(proj):
        r0 = s * sub
        rows = slice(r0, r0 + sub)
        u, va, pin, q, k, v = _split_proj(p)

        k2_ref[rows, :] = k.astype(BF16)
        for hp in range(HEAD_PAIRS):
            sl = slice(hp * LANES, (hp + 1) * LANES)
            qt_ref[0, hp, :, rows] = (q[:, sl] * (HEAD_DIM ** -0.5 * LOG2E)).T.astype(BF16)
            kout_ref[0, 0, hp, :, rows] = k[:, sl].T
            v_t = v[:, sl].T
            vout_ref[0, 0, hp, :, rows] = v_t
            vt_ref[0, hp, :, rows] = v_t.astype(BF16)

        lf_t = _log_sigmoid(fl + fb_ref[...]).T[0:C_HEADS, :]
        logf_ref[0, :, rows] = lf_t
        csum = _exact_rows_dot(lf_t, upto) + carry_ref[:, 0:1]
        carry_ref[...] = jnp.broadcast_to(csum[:, sub - 1:sub], carry_ref.shape)
        csum2 = csum * LOG2E
        crow_ref[0, :, rows] = csum2
        hi, mid, lo = _split3(csum2)
        cols = jnp.concatenate([hi.astype(F32), mid.astype(F32), lo.astype(F32),
                                jnp.zeros((LANES - 3 * C_HEADS, sub), F32)], axis=0)
        kc_ref[rows, :] = cols.T.astype(BF16)

        vn = _head_rms(va, cvg_ref[...])
        vnb = vn.astype(BF16)
        for ci in range(sub // CHUNK):
            rs = slice(ci * CHUNK, (ci + 1) * CHUNK)
            z = _gate_rows(w_heads, vnb[rs], bmat_ref[...])
            a_ref[r0 + ci * CHUNK:r0 + (ci + 1) * CHUNK, :] = (u[rs] * z).astype(BF16)

        ext_ref[POOL_CARRY + r0:POOL_CARRY + r0 + sub, :] = pin
        pos = i * tm + r0 + lax.broadcasted_iota(jnp.int32, (sub, LANES), 0)
        d_halves = []
        for half in range(2):
            ls = slice(half * LANES, (half + 1) * LANES)
            d_halves.append(_pool_delta(
                lambda j: ext_ref[pl.ds(POOL_CARRY + r0 - j, sub), ls],
                lambda w: jnp.minimum(pos + 1, w).astype(F32), half))
        d = jnp.concatenate(d_halves, axis=1).astype(BF16)
        b_ref[rows, :] = (_dot(d, pbd_ref[...]) * pscale_ref[...]).astype(BF16)

    tail = ext_ref[tm:tm + POOL_CARRY, :]
    tail_ref[0] = tail
    ext_ref[0:POOL_CARRY, :] = tail


def _proj_prompt(x, bsz, seq, gpre, win, wfl, fb, cvg, ws, bmat, pbd, pscale, k_all, v_all, layer, tm):
    n, d = x.shape
    nt = seq // tm
    dp = win.shape[1]
    row = lambda w: pl.BlockSpec((tm, w), lambda b, i: (b * nt + i, 0))
    pair_t = pl.BlockSpec((1, HEAD_PAIRS, LANES, tm), lambda b, i: (b, 0, 0, i))
    layer_pair_t = pl.BlockSpec((1, 1, HEAD_PAIRS, LANES, tm), lambda b, i: (layer, b, 0, 0, i))
    head_t = pl.BlockSpec((1, C_HEADS, tm), lambda b, i: (b, 0, i))
    pair_shape = (bsz, HEAD_PAIRS, LANES, seq)
    untouched = pl.BlockSpec(memory_space=pl.ANY)
    out_shapes = (
        jax.ShapeDtypeStruct((n, A_WIDTH), BF16),
        jax.ShapeDtypeStruct((n, B_WIDTH), BF16),
        jax.ShapeDtypeStruct(pair_shape, BF16),
        jax.ShapeDtypeStruct((n, C_WIDTH), BF16),
        jax.ShapeDtypeStruct((n, LANES), BF16),
        jax.ShapeDtypeStruct(pair_shape, BF16),
        jax.ShapeDtypeStruct(k_all.shape, F32),
        jax.ShapeDtypeStruct(v_all.shape, F32),
        jax.ShapeDtypeStruct((bsz, C_HEADS, seq), F32),
        jax.ShapeDtypeStruct((bsz, C_HEADS, seq), F32),
        jax.ShapeDtypeStruct((bsz, POOL_CARRY, B_WIDTH), F32),
    )
    out_specs = (
        row(A_WIDTH), row(B_WIDTH), pair_t, row(C_WIDTH), row(LANES), pair_t, layer_pair_t, layer_pair_t,
        head_t, head_t,
        pl.BlockSpec((1, POOL_CARRY, B_WIDTH), lambda b, i: (b, 0, 0)),
    )
    return pl.pallas_call(
        functools.partial(_proj_kernel, tm=tm),
        grid=(bsz, nt),
        in_specs=[row(d), _resident((1, d)), _resident((d, dp)), _resident((d, LANES)),
                  _resident((1, LANES)), _resident((1, A_WIDTH)), _resident((A_HEADS, CHUNK, CHUNK)),
                  _resident((CHUNK, A_WIDTH)), _resident((B_WIDTH, B_WIDTH)), _resident((1, B_WIDTH)),
                  untouched, untouched],
        out_specs=out_specs,
        out_shape=out_shapes,
        input_output_aliases={10: 6, 11: 7},
        scratch_shapes=[pltpu.VMEM((tm + POOL_CARRY, B_WIDTH), F32), pltpu.VMEM((C_HEADS, LANES), F32)],
        compiler_params=pltpu.CompilerParams(dimension_semantics=("arbitrary", "arbitrary"),
                                             vmem_limit_bytes=VMEM_LIMIT),
        name="proj_mix",
    )(x, gpre, win, wfl, fb, cvg, ws, bmat, pbd, pscale, k_all, v_all)


def _attn_kernel(qt_ref, k2_ref, kc_ref, vt_ref, cq_ref, o_ref, *, blk):
    hp = pl.program_id(1)
    qi = pl.program_id(2)
    q2 = qt_ref[0, 0]
    row = lax.broadcasted_iota(jnp.int32, (LANES, blk), 0)
    qa, cq = [], []
    for hh in range(2):
        own_q = jnp.where((row // HEAD_DIM) == hh, q2, jnp.zeros_like(q2))
        minus_ck = jnp.where((row < 3 * C_HEADS) & ((row % C_HEADS) == 2 * hp + hh), -1.0, 0.0)
        qa.append(jnp.concatenate([own_q, minus_ck.astype(BF16)], axis=0))
        cq.append(cq_ref[0, 0, hh:hh + 1, :])
    ones = jnp.ones((DENOM_ROWS, blk), BF16)

    def load(j):
        off = pl.multiple_of(j * blk, blk)
        kk = jnp.concatenate([k2_ref[pl.ds(off, blk), :], kc_ref[pl.ds(off, blk), :]], axis=1)
        va = [jnp.concatenate([vt_ref[0, 0, hh * HEAD_DIM:(hh + 1) * HEAD_DIM, pl.ds(off, blk)], ones], axis=0)
              for hh in range(2)]
        return kk, va

    def exact(j, carry, masked):
        kk, va = load(j)
        ts = [_dot(kk, qa[hh]) for hh in range(2)]
        new = []
        for hh in range(2):
            m_run, acc = carry[hh]
            t = ts[hh]
            if masked:
                kpos = lax.broadcasted_iota(jnp.int32, (blk, blk), 0)
                qpos = lax.broadcasted_iota(jnp.int32, (blk, blk), 1)
                t = jnp.where(kpos <= qpos, t, NEG_BIG)
            m_new = jnp.maximum(m_run, jnp.max(t, axis=0, keepdims=True) + cq[hh])
            pr = jnp.exp2(t - (m_new - cq[hh])).astype(BF16)
            new.append((m_new, jnp.exp2(m_run - m_new) * acc + _dot(va[hh], pr)))
        return tuple(new)

    init = tuple((jnp.full((1, blk), NEG_BIG, F32), jnp.zeros((HEAD_DIM + DENOM_ROWS, blk), F32))
                 for _ in range(2))

    def diagonal():
        kk, va = load(qi)
        strips = [(hh, s * DIAG_TILE, (s + 1) * DIAG_TILE) for hh in range(2) for s in range(blk // DIAG_TILE)]
        t_diag = [_dot(kk[lo:hi], qa[hh][:, lo:hi]) for hh, lo, hi in strips]
        t_old = [_dot(kk[0:lo], qa[hh][:, lo:hi]) if lo else None for hh, lo, hi in strips]
        kpos = lax.broadcasted_iota(jnp.int32, (DIAG_TILE, DIAG_TILE), 0)
        qpos = lax.broadcasted_iota(jnp.int32, (DIAG_TILE, DIAG_TILE), 1)
        m_tile, m_fin, p_diag, p_old, rise = [], [], [], [], []
        for (hh, lo, hi), td, to in zip(strips, t_diag, t_old):
            cq_s = cq[hh][:, lo:hi]
            td = jnp.where(kpos <= qpos, td, NEG_BIG)
            m_s = jnp.max(td, axis=0, keepdims=True) + cq_s
            p_diag.append(jnp.exp2(td - (m_s - cq_s)).astype(BF16))
            m_tile.append(m_s)
            if to is None:
                p_old.append(None)
                m_fin.append(m_s)
            else:
                p_old.append(jnp.exp2(to - (m_s - cq_s)).astype(BF16))
                m_fin.append(jnp.maximum(m_s, jnp.max(to, axis=0, keepdims=True) + cq_s))
                rise.append(m_fin[-1] - m_s)
        accs = []
        for (hh, lo, hi), pd, po, m_s, m_f in zip(strips, p_diag, p_old, m_tile, m_fin):
            acc_s = _dot(va[hh][:, lo:hi], pd)
            if po is not None:
                acc_s = jnp.exp2(m_s - m_f) * (acc_s + _dot(va[hh][:, 0:lo], po))
            accs.append(acc_s)
        per_head = blk // DIAG_TILE
        new = tuple((jnp.concatenate(m_fin[hh * per_head:(hh + 1) * per_head], axis=1),
                     jnp.concatenate(accs[hh * per_head:(hh + 1) * per_head], axis=1)) for hh in range(2))
        ok = jnp.all(jnp.concatenate(rise, axis=1) <= STALE_MAX_SLACK)
        return lax.cond(ok, lambda: new, lambda: exact(qi, init, True))

    carry = diagonal()

    def older(i, carry):
        j = qi - 1 - i
        kk, va = load(j)
        new, rise = [], []
        for hh in range(2):
            m_run, acc = carry[hh]
            t = _dot(kk, qa[hh])
            pr = jnp.exp2(t - (m_run - cq[hh])).astype(BF16)
            m_new = jnp.maximum(m_run, jnp.max(t, axis=0, keepdims=True) + cq[hh])
            new.append((m_new, jnp.exp2(m_run - m_new) * (acc + _dot(va[hh], pr))))
            rise.append(m_new - m_run)
        ok = jnp.all(jnp.concatenate(rise, axis=1) <= STALE_MAX_SLACK)
        return lax.cond(ok, lambda: tuple(new), lambda: exact(j, carry, False))

    carry = lax.fori_loop(0, qi, older, carry)
    out_t = jnp.concatenate([acc[0:HEAD_DIM] / acc[HEAD_DIM:HEAD_DIM + 1] for _, acc in carry], axis=0)
    o_ref[...] = out_t.T.astype(BF16)


def _attn_prompt(qt, k2, kc, vt, crow, bsz, seq, blk):
    n = bsz * seq
    nq = seq // blk
    crow4 = crow.reshape(bsz, HEAD_PAIRS, 2, seq)
    return pl.pallas_call(
        functools.partial(_attn_kernel, blk=blk),
        grid=(bsz, HEAD_PAIRS, nq),
        in_specs=[
            pl.BlockSpec((1, 1, LANES, blk), lambda b, hp, qi: (b, hp, 0, qi)),
            pl.BlockSpec((seq, LANES), lambda b, hp, qi: (b, hp)),
            pl.BlockSpec((seq, LANES), lambda b, hp, qi: (b, 0)),
            pl.BlockSpec((1, 1, LANES, seq), lambda b, hp, qi: (b, hp, 0, 0)),
            pl.BlockSpec((1, 1, 2, blk), lambda b, hp, qi: (b, hp, 0, qi)),
        ],
        out_specs=pl.BlockSpec((blk, LANES), lambda b, hp, qi: (b * nq + qi, hp)),
        out_shape=jax.ShapeDtypeStruct((n, C_WIDTH), BF16),
        compiler_params=pltpu.CompilerParams(
            dimension_semantics=("arbitrary", "arbitrary", "arbitrary"), vmem_limit_bytes=VMEM_LIMIT),
        name="forget_attn",
    )(qt, k2, kc, vt, crow4)


def _sproj_kernel(x_ref, gpre_ref, win_ref, wfl_ref, fb_ref, cvg_ref, wexp_ref, bsamp_ref, pbd_ref,
                  pscale_ref, state_ref,
                  a_ref, b_ref, q_ref, k_ref, v_ref, logf_ref, cnew_ref, vn_ref, pin_ref,
                  *, db, n_new, past):
    rows = db * n_new
    hb = _rms(x_ref[...], gpre_ref[...]).astype(BF16)
    p = _dot(hb, win_ref[...])
    fl = _dot(hb, wfl_ref[...])
    u, va, pin, q_ref[...], k_ref[...], v_ref[...] = _split_proj(p)
    pin_ref[...] = pin

    lf = _log_sigmoid(fl + fb_ref[...])
    logf_ref[...] = lf[:, 0:C_HEADS]
    run = lf[0:db]
    cnew_ref[0:db, :] = run[:, 0:C_HEADS]
    for t in range(1, n_new):
        run = run + lf[t * db:(t + 1) * db]
        cnew_ref[t * db:(t + 1) * db, :] = run[:, 0:C_HEADS]

    vn = _head_rms(va, cvg_ref[...])
    vn_ref[...] = vn
    rr = lax.broadcasted_iota(jnp.int32, (rows, rows), 0)
    cc = lax.broadcasted_iota(jnp.int32, (rows, rows), 1)
    keep = ((rr % db) == (cc % db)) & ((cc // db) <= (rr // db))
    w_heads = [jnp.where(keep, wexp_ref[h], 0.0).astype(BF16) for h in range(A_HEADS)]
    z = _gate_rows(w_heads, vn.astype(BF16), bsamp_ref[...])
    a_ref[...] = (u * z).astype(BF16)

    def ext(idx, ls):
        if idx < POOL_BUF:
            return state_ref[idx][:, ls]
        t = idx - POOL_BUF
        return pin[t * db:(t + 1) * db, ls]

    d_rows = []
    for t in range(n_new):
        d_halves = []
        for half in range(2):
            ls = slice(half * LANES, (half + 1) * LANES)
            d_halves.append(_pool_delta(lambda j: ext(POOL_BUF + t - j, ls),
                                        lambda w: float(min(past + t + 1, w)), half))
        d_rows.append(jnp.concatenate(d_halves, axis=1))
    d = jnp.concatenate(d_rows, axis=0).astype(BF16)
    b_ref[...] = (_dot(d, pbd_ref[...]) * pscale_ref[...]).astype(BF16)


def _proj_sample(x, gpre, win, wfl, fb, cvg, wexp, bsamp, pbd, pscale, state_t, db, n_new, past):
    rows, d = x.shape
    out_shapes = (
        jax.ShapeDtypeStruct((rows, A_WIDTH), BF16),
        jax.ShapeDtypeStruct((rows, B_WIDTH), BF16),
        jax.ShapeDtypeStruct((rows, C_WIDTH), F32),
        jax.ShapeDtypeStruct((rows, C_WIDTH), F32),
        jax.ShapeDtypeStruct((rows, C_WIDTH), F32),
        jax.ShapeDtypeStruct((rows, C_HEADS), F32),
        jax.ShapeDtypeStruct((rows, C_HEADS), F32),
        jax.ShapeDtypeStruct((rows, A_WIDTH), F32),
        jax.ShapeDtypeStruct((rows, B_WIDTH), F32),
    )
    args = (x, gpre, win, wfl, fb, cvg, wexp, bsamp, pbd, pscale, state_t)
    return pl.pallas_call(
        functools.partial(_sproj_kernel, db=db, n_new=n_new, past=past),
        grid=(1,),
        in_specs=[pl.BlockSpec(a.shape, lambda i, nd=a.ndim: (0,) * nd) for a in args],
        out_specs=tuple(pl.BlockSpec(s.shape, lambda i: (0, 0)) for s in out_shapes),
        out_shape=out_shapes,
        compiler_params=pltpu.CompilerParams(dimension_semantics=("arbitrary",),
                                             vmem_limit_bytes=VMEM_LIMIT),
        name="proj_mix_sample",
    )(*args)


def _pattn_kernel(pt_ref, q_ref, knew_ref, vnew_ref, cq_ref, cknew_ref, kt_hbm, vt_hbm, lft_hbm,
                  o_ref, kbuf, vbuf, lfbuf, sems, *, layer, group, n_pages, n_new):
    b = pl.program_id(0)
    n_seq = pl.num_programs(0)
    n_steps = n_pages // group
    rows = n_new * C_HEADS
    page = lfbuf.shape[-1]

    def copies(seq, step, slot):
        out = []
        for i in range(group):
            phys = pt_ref[seq, n_pages - 1 - (step * group + i)]
            out.append(pltpu.make_async_copy(kt_hbm.at[layer, phys], kbuf.at[slot, i], sems.at[slot, 0]))
            out.append(pltpu.make_async_copy(vt_hbm.at[layer, phys], vbuf.at[slot, i], sems.at[slot, 1]))
            out.append(pltpu.make_async_copy(lft_hbm.at[layer, phys], lfbuf.at[slot, i], sems.at[slot, 2]))
        return out

    @pl.when(b == 0)
    def _():
        for cp in copies(0, 0, 0):
            cp.start()

    row_head = lax.broadcasted_iota(jnp.int32, (rows, C_WIDTH), 0) % C_HEADS
    lane_head = lax.broadcasted_iota(jnp.int32, (rows, C_WIDTH), 1) // HEAD_DIM
    own = row_head == lane_head
    qs = q_ref[0] * (HEAD_DIM ** -0.5)
    qrep = jnp.concatenate([jnp.broadcast_to(qs[t:t + 1], (C_HEADS, C_WIDTH)) for t in range(n_new)], axis=0)
    qbd = jnp.where(own, qrep, 0.0).astype(BF16)
    cq = cq_ref[0]
    sr = lax.broadcasted_iota(jnp.int32, (page, 2 * page), 0)
    sc = lax.broadcasted_iota(jnp.int32, (page, 2 * page), 1)
    later_or_all = jnp.where((sr > sc) | (sc >= page), 1.0, 0.0).astype(BF16)

    def update(state, scores, pv):
        m_run, l_run, acc = state
        m_cur = m_run
        for s in scores:
            m_cur = jnp.maximum(m_cur, jnp.max(s, axis=1, keepdims=True))
        alpha = jnp.exp(m_run - m_cur)
        l_new = alpha * l_run
        acc = alpha * acc
        for i, s in enumerate(scores):
            pr = jnp.exp(s - m_cur)
            l_new = l_new + jnp.sum(pr, axis=1, keepdims=True)
            acc = acc + pv(i, pr.astype(BF16))
        return m_cur, l_new, acc

    def step(s, carry):
        state, later = carry
        slot = (b * n_steps + s) % 2

        @pl.when(s + 1 < n_steps)
        def _():
            for cp in copies(b, s + 1, 1 - slot):
                cp.start()

        @pl.when((s + 1 == n_steps) & (b + 1 < n_seq))
        def _():
            for cp in copies(b + 1, 0, 1 - slot):
                cp.start()

        for cp in copies(b, s, slot):
            cp.wait()

        sums = _exact_rows_dot(jnp.concatenate([lfbuf[slot, i] for i in range(group)], axis=0), later_or_all)
        scores = []
        for i in range(group):
            hs = slice(i * C_HEADS, (i + 1) * C_HEADS)
            bias = sums[hs, 0:page] + later
            later = later + sums[hs, page:2 * page]
            sco = _dot(qbd, kbuf[slot, i].astype(BF16))
            scores.append(sco + jnp.concatenate([bias] * n_new, axis=0) + cq)
        state = update(state, scores, lambda i, pr: _dot_nt(pr, vbuf[slot, i].astype(BF16)))
        return state, later

    init = ((jnp.full((rows, 1), NEG_BIG, F32), jnp.zeros((rows, 1), F32), jnp.zeros((rows, C_WIDTH), F32)),
            jnp.zeros((C_HEADS, page), F32))
    state, _ = lax.fori_loop(0, n_steps, step, init)

    s_new = _dot_nt(qbd, knew_ref[0].astype(BF16))
    s_new = s_new + cq[:, 0:NEW_PAD] - cknew_ref[0]
    tq = lax.broadcasted_iota(jnp.int32, (rows, NEW_PAD), 0) // C_HEADS
    sk = lax.broadcasted_iota(jnp.int32, (rows, NEW_PAD), 1)
    s_new = jnp.where(sk <= tq, s_new, NEG_BIG)
    _, l_fin, acc = update(state, [s_new], lambda i, pr: _dot(pr, vnew_ref[0].astype(BF16)))
    full = jnp.where(own, acc / l_fin, 0.0)
    o_ref[0] = jnp.sum(full.reshape(n_new, C_HEADS, C_WIDTH), axis=1).astype(BF16)


def _paged_attn(page_table, q_b, knew_b, vnew_b, cq_rep, cknew, cache_kt, cache_vt, cache_lft, layer, group):
    db, n_pages = page_table.shape
    n_new = q_b.shape[1]
    page = cache_kt.shape[-1]
    rows = n_new * C_HEADS
    per_seq = lambda shape: pl.BlockSpec((1,) + shape, lambda b, pt: (b,) + (0,) * len(shape))
    in_hbm = pl.BlockSpec(memory_space=pl.ANY)
    grid_spec = pltpu.PrefetchScalarGridSpec(
        num_scalar_prefetch=1,
        grid=(db,),
        in_specs=[per_seq((n_new, C_WIDTH)), per_seq(knew_b.shape[1:]), per_seq(vnew_b.shape[1:]),
                  per_seq((rows, LANES)), per_seq(cknew.shape[1:]), in_hbm, in_hbm, in_hbm],
        out_specs=pl.BlockSpec((1, n_new, C_WIDTH), lambda b, pt: (b, 0, 0)),
        scratch_shapes=[pltpu.VMEM((2, group, C_WIDTH, page), F32), pltpu.VMEM((2, group, C_WIDTH, page), F32),
                        pltpu.VMEM((2, group, C_HEADS, page), F32), pltpu.SemaphoreType.DMA((2, 3))],
    )
    return pl.pallas_call(
        functools.partial(_pattn_kernel, layer=layer, group=group, n_pages=n_pages, n_new=n_new),
        grid_spec=grid_spec,
        out_shape=jax.ShapeDtypeStruct((db, n_new, C_WIDTH), BF16),
        compiler_params=pltpu.CompilerParams(dimension_semantics=("arbitrary",),
                                             vmem_limit_bytes=VMEM_LIMIT),
        name="paged_forget_attn",
    )(page_table, q_b, knew_b, vnew_b, cq_rep, cknew, cache_kt, cache_vt, cache_lft)


def _row_tile(n, target):
    t = min(n, target)
    while n % t:
        t //= 2
    return t


def kernel(x_prompt, x_sample, cache_k, cache_v, cache_logf, state_pool, page_table,
           ffn1_g_pre, ffn1_g_post, ffn1_w_gate, ffn1_w_up, ffn1_w_down,
           mix_g_pre, mix_g_post, w_in, w_out,
           chunk_v_g, chunk_w_s, chunk_b, pool_w, pool_scale, forget_b,
           ffn2_g_pre, ffn2_g_post, ffn2_w_gate, ffn2_w_up, ffn2_w_down):
    bsz, seq, d = x_prompt.shape
    db, n_new, _ = x_sample.shape
    depth, n_phys, page = cache_k.shape[:3]
    n_pages = page_table.shape[1]
    past = n_pages * page
    n = bsz * seq
    rows_s = db * n_new
    d_main = w_in.shape[-1] - C_HEADS

    tm_ffn = _row_tile(n, FFN_ROWS)
    tm_proj = _row_tile(seq, PROJ_ROWS)
    blk = _row_tile(seq, ATTN_BLOCK)
    group = _row_tile(n_pages, PAGE_GROUP)

    xp = x_prompt.reshape(n, d)
    xs = jnp.transpose(x_sample, (1, 0, 2)).reshape(rows_s, d)
    cache_kt = jnp.transpose(cache_k, (0, 1, 3, 4, 2)).reshape(depth, n_phys, C_WIDTH, page)
    cache_vt = jnp.transpose(cache_v, (0, 1, 3, 4, 2)).reshape(depth, n_phys, C_WIDTH, page)
    cache_lft = jnp.transpose(cache_logf, (0, 1, 3, 2))

    outs = {k: [] for k in ("lfp", "poolp", "ks", "vs", "lfs", "pools", "cvs")}
    k_all = jnp.zeros((depth, bsz, HEAD_PAIRS, LANES, seq), F32)
    v_all = jnp.zeros((depth, bsz, HEAD_PAIRS, LANES, seq), F32)
    ffn1_w = tuple(w.astype(BF16) for w in (ffn1_w_gate, ffn1_w_up, ffn1_w_down))
    ffn2_w = tuple(w.astype(BF16) for w in (ffn2_w_gate, ffn2_w_up, ffn2_w_down))
    wout = w_out.astype(BF16)
    for l in range(depth):
        row2 = lambda v: v[l].reshape(1, -1)
        f1 = (row2(ffn1_g_pre), row2(ffn1_g_post)) + ffn1_w + (l,)
        f2 = (row2(ffn2_g_pre), row2(ffn2_g_post)) + ffn2_w + (l,)
        win = w_in[l, :, :d_main].astype(BF16)
        wfl = jnp.pad(w_in[l, :, d_main:], ((0, 0), (0, LANES - C_HEADS))).astype(BF16)
        fb = jnp.pad(forget_b[l], (0, LANES - C_HEADS)).reshape(1, LANES)
        bmat = jnp.repeat(chunk_b[l].T, HEAD_DIM, axis=1)
        pbd = jax.scipy.linalg.block_diag(*[pool_w[l, g] for g in range(len(POOL_WINDOWS))]).astype(BF16)
        mixp = (row2(mix_g_pre), win, wfl, fb, row2(chunk_v_g))
        pool_p = (pbd, row2(pool_scale))

        xp = _ffn(xp, *f1, tm_ffn)
        (a, b, qt, k2, kc, vt, k_all, v_all, lf_t, crow, tail) = _proj_prompt(
            xp, bsz, seq, *mixp, chunk_w_s[l], bmat, *pool_p, k_all, v_all, l, tm_proj)
        c = _attn_prompt(qt, k2, kc, vt, crow, bsz, seq, blk)
        xp = _mix_ffn(xp, a, b, c, wout, row2(mix_g_post), *f2, tm_ffn)
        outs["lfp"].append(jnp.transpose(lf_t, (0, 2, 1)))
        outs["poolp"].append(tail[:, POOL_CARRY - POOL_BUF:])

        xs = _ffn(xs, *f1, rows_s)
        wexp = jnp.repeat(jnp.repeat(chunk_w_s[l, :, :n_new, :n_new], db, axis=1), db, axis=2)
        bsamp = jnp.repeat(bmat[:n_new], db, axis=0)
        state_t = jnp.transpose(state_pool[l], (1, 0, 2))
        (a_s, b_s, q_s, k_s, v_s, lf_s, cnew, vn_s, pin_s) = _proj_sample(
            xs, *mixp, wexp, bsamp, *pool_p, state_t, db, n_new, past)
        to_b = lambda v: jnp.transpose(v.reshape(n_new, db, -1), (1, 0, 2))
        q_b, k_b, v_b, cnew_b = to_b(q_s), to_b(k_s), to_b(v_s), to_b(cnew)
        pad_new = ((0, 0), (0, NEW_PAD - n_new), (0, 0))
        cq_rep = jnp.broadcast_to(cnew_b.reshape(db, n_new * C_HEADS, 1), (db, n_new * C_HEADS, LANES))
        cknew = jnp.pad(jnp.tile(jnp.transpose(cnew_b, (0, 2, 1)), (1, n_new, 1)),
                        ((0, 0), (0, 0), (0, NEW_PAD - n_new)))
        c_s = _paged_attn(page_table, q_b, jnp.pad(k_b, pad_new), jnp.pad(v_b, pad_new), cq_rep, cknew,
                          cache_kt, cache_vt, cache_lft, l, group)
        c_s = jnp.transpose(c_s, (1, 0, 2)).reshape(rows_s, C_WIDTH)
        xs = _mix_ffn(xs, a_s, b_s, c_s, wout, row2(mix_g_post), *f2, rows_s)
        outs["ks"].append(k_b.reshape(db, n_new, C_HEADS, HEAD_DIM))
        outs["vs"].append(v_b.reshape(db, n_new, C_HEADS, HEAD_DIM))
        outs["lfs"].append(to_b(lf_s))
        outs["pools"].append(jnp.concatenate([state_pool[l], to_b(pin_s)], axis=1)[:, -POOL_BUF:])
        outs["cvs"].append(to_b(vn_s))

    y_prompt = xp.reshape(bsz, seq, d)
    y_sample = jnp.transpose(xs.reshape(n_new, db, d), (1, 0, 2))
    st = lambda key: jnp.stack(outs[key])
    to_heads = lambda t: jnp.transpose(t.reshape(depth, bsz, C_HEADS, HEAD_DIM, seq), (0, 1, 4, 2, 3))
    return (y_prompt, y_sample, to_heads(k_all), to_heads(v_all), st("lfp"), st("poolp"),
            st("ks"), st("vs"), st("lfs"), st("pools"), st("cvs"))
```
